```python
import jax, jax.numpy as jnp
from jax import lax
import numpy as np

D_MODEL = 4096
BATCH = 4
SEQ = 2048
DEPTH = 4
DEC_BATCH = 128
DEC_SEQ = 4
PAST_LEN = 16384
PAGE_SIZE = 128

N_MIXERS = 2
N_A_LAYERS = (DEPTH + 1) // 2
N_B_LAYERS = DEPTH // 2
D_RNN = D_MODEL
LRU_BLOCKS = 16
LRU_BLOCK_W = D_RNN // LRU_BLOCKS
CONV_W = 4
LRU_C = 8.0
RWKV_HEAD = 64
RWKV_HEADS = D_MODEL // RWKV_HEAD
D_DECAY_LORA = max(32, int(round(1.8 * D_MODEL ** 0.5 / 32)) * 32)
D_AAA_LORA = max(32, int(round(1.8 * D_MODEL ** 0.5 / 32)) * 32)
D_MV_LORA = max(32, int(round(1.3 * D_MODEL ** 0.5 / 32)) * 32)
D_GATE_LORA = max(32, int(round(0.6 * D_MODEL ** 0.8 / 32)) * 32)
GN_EPS = 64e-5
N_GROUPS = 4
EXPERTS_PER_GROUP = 8
N_EXPERTS = N_GROUPS * EXPERTS_PER_GROUP
TOP_K = 2
D_EXPERT = D_MODEL // 4
MOE_BLOCK = 128
DN_ALPHA = (2 * DEPTH) ** 0.25
DN_BETA = (8 * DEPTH) ** -0.25
LN_EPS = 1e-5

kernel_name = 'hybrid_rglru_rwkv7_hmoe_step'


def layer_norm(x, g, b):
    xf = x.astype(jnp.float32)
    mu = jnp.mean(xf, axis=-1, keepdims=True)
    var = jnp.mean(jnp.square(xf - mu), axis=-1, keepdims=True)
    return ((xf - mu) * lax.rsqrt(var + LN_EPS) * g + b).astype(x.dtype)


def lru_scan(a, b, h0):
    b = b.at[:, 0].add(a[:, 0] * h0)
    def combine(left, right):
        a_l, b_l = left
        a_r, b_r = right
        return a_l * a_r, a_r * b_l + b_r
    _, h = lax.associative_scan(combine, (a, b), axis=1)
    return h


def rglru_mixer(x, conv_buf, h0, w_in, conv_w, conv_b, w_a, b_a, w_x, b_x, lam, w_out):
    bsz, t_len, _ = x.shape
    proj = x @ w_in
    xb, gate = proj[..., :D_RNN], proj[..., D_RNN:]
    xc_in = jnp.concatenate([conv_buf.astype(xb.dtype), xb], axis=1)
    xc = conv_b + xc_in[:, 0:t_len] * conv_w[0]
    for j in range(1, CONV_W):
        xc = xc + xc_in[:, j:j + t_len] * conv_w[j]
    new_buf = xc_in[:, t_len:]
    xh = xc.reshape(bsz, t_len, LRU_BLOCKS, LRU_BLOCK_W)
    r = jax.nn.sigmoid(jnp.einsum('btnc,ncd->btnd', xh, w_a) + b_a.reshape(LRU_BLOCKS, LRU_BLOCK_W))
    i = jax.nn.sigmoid(jnp.einsum('btnc,ncd->btnd', xh, w_x) + b_x.reshape(LRU_BLOCKS, LRU_BLOCK_W))
    log_a = -LRU_C * r.astype(jnp.float32) * jax.nn.softplus(-lam.astype(jnp.float32)).reshape(LRU_BLOCKS, LRU_BLOCK_W)
    a = jnp.exp(log_a)
    mult = jnp.sqrt(-jnp.expm1(2.0 * log_a))
    bvec = mult * (i * xh).astype(jnp.float32)
    h = lru_scan(a.reshape(bsz, t_len, D_RNN), bvec.reshape(bsz, t_len, D_RNN), h0.astype(jnp.float32))
    y = h.astype(x.dtype) * jax.nn.gelu(gate, approximate=True)
    return y @ w_out, new_buf, h[:, -1]


def wkv7_scan(S0, r, decay, k, v, kk, ka):
    def step(S, inp):
        r_t, w_t, k_t, v_t, kk_t, ka_t = inp
        sa = jnp.einsum('bhvk,bhk->bhv', S, -kk_t)
        S = S * w_t[:, :, None, :] + sa[..., None] * ka_t[:, :, None, :] + v_t[..., None] * k_t[:, :, None, :]
        return S, jnp.einsum('bhvk,bhk->bhv', S, r_t)
    xs = tuple(jnp.moveaxis(t, 1, 0) for t in (r, decay, k, v, kk, ka))
    S, y = lax.scan(step, S0, xs)
    return jnp.moveaxis(y, 0, 1), S


def rwkv7_mixer(x, shift0, S0, v_first, vres, mu, w_r, w_k, w_v, w_o, w0, w1, w2, a0, a1, a2,
                g1, g2, k_k, k_a, r_k, lnx_g, lnx_b):
    bsz, t_len, d = x.shape
    f32 = jnp.float32
    prev = jnp.concatenate([shift0[:, None].astype(x.dtype), x[:, :-1]], axis=1)
    xx = prev - x
    xr, xw, xk, xv, xa, xg = (x + xx * mu[j] for j in range(6))
    r = xr @ w_r
    w = -jax.nn.softplus(-(w0 + jnp.tanh(xw @ w1) @ w2)) - 0.5
    k = xk @ w_k
    v = xv @ w_v
    if vres is None:
        v_first = v
    else:
        v0, v1, v2 = vres
        v = v + (v_first - v) * jax.nn.sigmoid(v0 + (xv @ v1) @ v2)
    a = jax.nn.sigmoid(a0 + (xa @ a1) @ a2)
    g = jax.nn.sigmoid(xg @ g1) @ g2
    hs = (bsz, t_len, RWKV_HEADS, RWKV_HEAD)
    kk = (k * k_k).astype(f32).reshape(hs)
    kk = kk / jnp.maximum(jnp.linalg.norm(kk, axis=-1, keepdims=True), 1e-12)
    k = k * (1.0 + (a - 1.0) * k_a)
    decay = jnp.exp(-jnp.exp(w.astype(f32)))
    ka = kk * a.astype(f32).reshape(hs)
    rh = r.astype(f32).reshape(hs)
    kh = k.astype(f32).reshape(hs)
    vh = v.astype(f32).reshape(hs)
    y, S = wkv7_scan(S0.astype(f32), rh, decay.reshape(hs), kh, vh, kk, ka)
    mu_y = jnp.mean(y, axis=-1, keepdims=True)
    var_y = jnp.mean(jnp.square(y - mu_y), axis=-1, keepdims=True)
    y = ((y - mu_y) * lax.rsqrt(var_y + GN_EPS)).reshape(bsz, t_len, d) * lnx_g + lnx_b
    bonus = jnp.sum(rh * kh * r_k, axis=-1, keepdims=True) * vh
    y = (y + bonus.reshape(bsz, t_len, d)).astype(x.dtype)
    return (y * g) @ w_o, x[:, -1], S, v_first


def grouped_expert_ffn(xt, expert_flat, gate_flat, w_gate, w_up, w_down):
    n_tok, d = xt.shape
    n_assign = expert_flat.shape[0]
    tok_flat = jnp.arange(n_assign, dtype=jnp.int32) // TOP_K
    order = jnp.argsort(expert_flat)
    e_sorted = expert_flat[order]
    tok_sorted = tok_flat[order]
    gate_sorted = gate_flat[order]
    counts = jnp.bincount(expert_flat, length=N_EXPERTS)
    start = jnp.cumsum(counts) - counts
    padded = (counts + MOE_BLOCK - 1) // MOE_BLOCK * MOE_BLOCK
    pad_end = jnp.cumsum(padded)
    pad_start = pad_end - padded
    slot = pad_start[e_sorted] + (jnp.arange(n_assign, dtype=jnp.int32) - start[e_sorted])
    n_blocks = (n_assign + N_EXPERTS * (MOE_BLOCK - 1) + MOE_BLOCK - 1) // MOE_BLOCK
    n_slots = n_blocks * MOE_BLOCK
    slot_tok = jnp.full((n_slots,), n_tok, jnp.int32).at[slot].set(tok_sorted)
    block_start = jnp.arange(n_blocks, dtype=pad_end.dtype) * MOE_BLOCK
    block_expert = jnp.minimum(jnp.searchsorted(pad_end, block_start, side='right'), N_EXPERTS - 1)
    x_pad = jnp.concatenate([xt, jnp.zeros((1, d), xt.dtype)], axis=0)

    def run_block(args):
        toks, e = args
        xb = x_pad[toks]
        hid = jax.nn.silu(xb @ w_gate[e]) * (xb @ w_up[e])
        return hid @ w_down[e]

    out_slots = lax.map(run_block, (slot_tok.reshape(n_blocks, MOE_BLOCK), block_expert))
    out_sorted = out_slots.reshape(n_slots, d)[slot]
    weighted = out_sorted * gate_sorted.astype(out_sorted.dtype)[:, None]
    return jax.ops.segment_sum(weighted, tok_sorted, num_segments=n_tok)


def hier_moe(x, w_group, b_group, w_expert, b_expert, w_gate, w_up, w_down):
    bsz, t_len, d = x.shape
    xt = x.reshape(bsz * t_len, d)
    g_logits = (xt @ w_group + b_group).astype(jnp.float32)
    g_idx = jnp.argmax(g_logits, axis=-1)
    p_group = jnp.max(jax.nn.softmax(g_logits, axis=-1), axis=-1, keepdims=True)
    e_logits = (xt @ w_expert + b_expert).astype(jnp.float32).reshape(-1, N_GROUPS, EXPERTS_PER_GROUP)
    e_in = jnp.einsum('nge,ng->ne', e_logits, jax.nn.one_hot(g_idx, N_GROUPS, dtype=jnp.float32))
    top_val, top_idx = lax.top_k(e_in, TOP_K)
    gate = p_group * jax.nn.softmax(top_val, axis=-1)
    expert_id = (g_idx[:, None] * EXPERTS_PER_GROUP + top_idx).astype(jnp.int32)
    y = grouped_expert_ffn(xt, expert_id.reshape(-1), gate.reshape(-1), w_gate, w_up, w_down)
    return y.reshape(bsz, t_len, d)


def trunk(x, conv_st, h_st, shift_st, wkv_st, p):
    new_conv, new_h, new_shift, new_wkv = [], [], [], []
    v_first = None
    for i in range(DEPTH):
        j = i // N_MIXERS
        if i % N_MIXERS == 0:
            mix, c_new, h_new = rglru_mixer(
                x, conv_st[j], h_st[j], p['lru_w_in'][j], p['lru_conv_w'][j], p['lru_conv_b'][j],
                p['lru_w_a'][j], p['lru_b_a'][j], p['lru_w_x'][j], p['lru_b_x'][j], p['lru_lam'][j],
                p['lru_w_out'][j])
            new_conv.append(c_new.astype(conv_st.dtype))
            new_h.append(h_new.astype(h_st.dtype))
        else:
            vres = None if j == 0 else (p['rw_v0'][j - 1], p['rw_v1'][j - 1], p['rw_v2'][j - 1])
            mix, s_new, S_new, v_first = rwkv7_mixer(
                x, shift_st[j], wkv_st[j], v_first, vres, p['rw_mu'][j], p['rw_w_r'][j], p['rw_w_k'][j],
                p['rw_w_v'][j], p['rw_w_o'][j], p['rw_w0'][j], p['rw_w1'][j], p['rw_w2'][j],
                p['rw_a0'][j], p['rw_a1'][j], p['rw_a2'][j], p['rw_g1'][j], p['rw_g2'][j],
                p['rw_k_k'][j], p['rw_k_a'][j], p['rw_r_k'][j], p['rw_lnx_g'][j], p['rw_lnx_b'][j])
            new_shift.append(s_new.astype(shift_st.dtype))
            new_wkv.append(S_new.astype(wkv_st.dtype))
        x = layer_norm(DN_ALPHA * x + mix, p['ln_g'][i, 0], p['ln_b'][i, 0])
        ffn = hier_moe(x, p['moe_w_group'][i], p['moe_b_group'][i], p['moe_w_expert'][i], p['moe_b_expert'][i],
                       p['moe_w_gate'][i], p['moe_w_up'][i], p['moe_w_down'][i])
        x = layer_norm(DN_ALPHA * x + ffn, p['ln_g'][i, 1], p['ln_b'][i, 1])
    return x, jnp.stack(new_conv), jnp.stack(new_h), jnp.stack(new_shift), jnp.stack(new_wkv)


def setup_inputs(seed: int = 0) -> dict:
    key = jax.random.key(seed)
    ks = iter(jax.random.split(key, 64))
    f32 = jnp.float32

    def nrm(shape, scale):
        return jax.random.normal(next(ks), shape, f32) * scale

    def uni(shape, lo, hi):
        return jax.random.uniform(next(ks), shape, f32, lo, hi)

    NA, NB = N_A_LAYERS, N_B_LAYERS
    H, N = RWKV_HEADS, RWKV_HEAD
    sd = D_MODEL ** -0.5
    a_init = uni((NA, D_RNN), 0.9, 0.999)
    s = a_init ** (1.0 / LRU_C)
    lru_lam = jnp.log(s) - jnp.log1p(-s)
    return {
        'x_prompt': nrm((BATCH, SEQ, D_MODEL), 1.0),
        'x_sample': nrm((DEC_BATCH, DEC_SEQ, D_MODEL), 1.0),
        'state_conv': nrm((NA, DEC_BATCH, CONV_W - 1, D_RNN), 1.0),
        'state_h': nrm((NA, DEC_BATCH, D_RNN), 0.5),
        'state_shift': nrm((NB, DEC_BATCH, D_MODEL), 1.0),
        'state_wkv': nrm((NB, DEC_BATCH, H, N, N), 0.5),
        'ln_g': 1.0 + nrm((DEPTH, 2, D_MODEL), 0.02),
        'ln_b': nrm((DEPTH, 2, D_MODEL), 0.02),
        'lru_w_in': nrm((NA, D_MODEL, 2 * D_RNN), sd),
        'lru_conv_w': nrm((NA, CONV_W, D_RNN), CONV_W ** -0.5),
        'lru_conv_b': nrm((NA, D_RNN), 0.01),
        'lru_w_a': nrm((NA, LRU_BLOCKS, LRU_BLOCK_W, LRU_BLOCK_W), LRU_BLOCK_W ** -0.5),
        'lru_b_a': nrm((NA, D_RNN), 0.1),
        'lru_w_x': nrm((NA, LRU_BLOCKS, LRU_BLOCK_W, LRU_BLOCK_W), LRU_BLOCK_W ** -0.5),
        'lru_b_x': nrm((NA, D_RNN), 0.1),
        'lru_lam': lru_lam,
        'lru_w_out': nrm((NA, D_RNN, D_MODEL), DN_BETA * D_RNN ** -0.5),
        'rw_mu': uni((NB, 6, D_MODEL), 0.0, 1.0),
        'rw_w_r': nrm((NB, D_MODEL, D_MODEL), sd),
        'rw_w_k': nrm((NB, D_MODEL, D_MODEL), sd),
        'rw_w_v': nrm((NB, D_MODEL, D_MODEL), sd),
        'rw_w_o': nrm((NB, D_MODEL, D_MODEL), DN_BETA * sd),
        'rw_w0': uni((NB, D_MODEL), -5.0, 0.5),
        'rw_w1': nrm((NB, D_MODEL, D_DECAY_LORA), 0.5 * sd),
        'rw_w2': nrm((NB, D_DECAY_LORA, D_MODEL), 0.5 * D_DECAY_LORA ** -0.5),
        'rw_a0': nrm((NB, D_MODEL), 0.1),
        'rw_a1': nrm((NB, D_MODEL, D_AAA_LORA), sd),
        'rw_a2': nrm((NB, D_AAA_LORA, D_MODEL), 0.5 * D_AAA_LORA ** -0.5),
        'rw_v0': nrm((NB - 1, D_MODEL), 0.1),
        'rw_v1': nrm((NB - 1, D_MODEL, D_MV_LORA), sd),
        'rw_v2': nrm((NB - 1, D_MV_LORA, D_MODEL), 0.5 * D_MV_LORA ** -0.5),
        'rw_g1': nrm((NB, D_MODEL, D_GATE_LORA), sd),
        'rw_g2': nrm((NB, D_GATE_LORA, D_MODEL), D_GATE_LORA ** -0.5),
        'rw_k_k': 0.85 + nrm((NB, D_MODEL), 0.02),
        'rw_k_a': 1.0 + nrm((NB, D_MODEL), 0.02),
        'rw_r_k': nrm((NB, H, N), 0.1),
        'rw_lnx_g': 1.0 + nrm((NB, D_MODEL), 0.02),
        'rw_lnx_b': nrm((NB, D_MODEL), 0.02),
        'moe_w_group': nrm((DEPTH, D_MODEL, N_GROUPS), sd),
        'moe_b_group': nrm((DEPTH, N_GROUPS), 0.01),
        'moe_w_expert': nrm((DEPTH, D_MODEL, N_EXPERTS), sd),
        'moe_b_expert': nrm((DEPTH, N_EXPERTS), 0.01),
        'moe_w_gate': nrm((DEPTH, N_EXPERTS, D_MODEL, D_EXPERT), sd),
        'moe_w_up': nrm((DEPTH, N_EXPERTS, D_MODEL, D_EXPERT), sd),
        'moe_w_down': nrm((DEPTH, N_EXPERTS, D_EXPERT, D_MODEL), DN_BETA * D_EXPERT ** -0.5),
    }


def reference(x_prompt, x_sample, state_conv, state_h, state_shift, state_wkv, ln_g, ln_b,
              lru_w_in, lru_conv_w, lru_conv_b, lru_w_a, lru_b_a, lru_w_x, lru_b_x, lru_lam, lru_w_out,
              rw_mu, rw_w_r, rw_w_k, rw_w_v, rw_w_o, rw_w0, rw_w1, rw_w2, rw_a0, rw_a1, rw_a2,
              rw_v0, rw_v1, rw_v2, rw_g1, rw_g2, rw_k_k, rw_k_a, rw_r_k, rw_lnx_g, rw_lnx_b,
              moe_w_group, moe_b_group, moe_w_expert, moe_b_expert, moe_w_gate, moe_w_up, moe_w_down):
    p = dict(ln_g=ln_g, ln_b=ln_b, lru_w_in=lru_w_in, lru_conv_w=lru_conv_w, lru_conv_b=lru_conv_b,
             lru_w_a=lru_w_a, lru_b_a=lru_b_a, lru_w_x=lru_w_x, lru_b_x=lru_b_x, lru_lam=lru_lam,
             lru_w_out=lru_w_out, rw_mu=rw_mu, rw_w_r=rw_w_r, rw_w_k=rw_w_k, rw_w_v=rw_w_v, rw_w_o=rw_w_o,
             rw_w0=rw_w0, rw_w1=rw_w1, rw_w2=rw_w2, rw_a0=rw_a0, rw_a1=rw_a1, rw_a2=rw_a2,
             rw_v0=rw_v0, rw_v1=rw_v1, rw_v2=rw_v2, rw_g1=rw_g1, rw_g2=rw_g2, rw_k_k=rw_k_k, rw_k_a=rw_k_a,
             rw_r_k=rw_r_k, rw_lnx_g=rw_lnx_g, rw_lnx_b=rw_lnx_b, moe_w_group=moe_w_group,
             moe_b_group=moe_b_group, moe_w_expert=moe_w_expert, moe_b_expert=moe_b_expert,
             moe_w_gate=moe_w_gate, moe_w_up=moe_w_up, moe_w_down=moe_w_down)
    dt = x_prompt.dtype
    conv0 = jnp.zeros((N_A_LAYERS, BATCH, CONV_W - 1, D_RNN), dt)
    h0 = jnp.zeros((N_A_LAYERS, BATCH, D_RNN), dt)
    shift0 = jnp.zeros((N_B_LAYERS, BATCH, D_MODEL), dt)
    wkv0 = jnp.zeros((N_B_LAYERS, BATCH, RWKV_HEADS, RWKV_HEAD, RWKV_HEAD), dt)
    y_prompt, conv_p, h_p, shift_p, wkv_p = trunk(x_prompt, conv0, h0, shift0, wkv0, p)
    y_sample, conv_s, h_s, shift_s, wkv_s = trunk(x_sample, state_conv, state_h, state_shift, state_wkv, p)
    return (y_prompt, y_sample, conv_p, h_p, shift_p, wkv_p, conv_s, h_s, shift_s, wkv_s)
```

```python
import functools

import jax
import jax.numpy as jnp
from jax import lax
from jax.experimental import pallas as pl
from jax.experimental.pallas import tpu as pltpu

F32 = jnp.float32
BF16 = jnp.bfloat16

V7X_VMEM_LIMIT_BYTES = 56 * 1024 * 1024
LANES = 128
SUBLANES = 8
RWKV_HEAD = 64
PAIR = 2 * RWKV_HEAD
CONV_W = 4
LRU_C = 8.0
GN_EPS = 64e-5
LN_EPS = 1e-5
TOP_K = 2
ROUTE_PAD = LANES


def _params(*sem):
    return pltpu.CompilerParams(dimension_semantics=sem, vmem_limit_bytes=V7X_VMEM_LIMIT_BYTES)


def _sigmoid(x):
    return 1.0 / (1.0 + jnp.exp(-x))


def _softplus(x):
    return jnp.maximum(x, 0.0) + jnp.log1p(jnp.exp(-jnp.abs(x)))


def _gelu_tanh(x):
    return 0.5 * x * (1.0 + jnp.tanh(0.7978845608028654 * (x + 0.044715 * (x * x * x))))


def _row_tile(mp, ms, cap):
    tm = min(cap, ms)
    assert mp % tm == 0 and ms % tm == 0 and tm % SUBLANES == 0, (mp, ms, tm)
    return tm


def _div_tile(m, cap):
    tm = min(cap, m) // SUBLANES * SUBLANES
    while m % tm:
        tm -= SUBLANES
    return tm


def _mm_body(xp_ref, xs_ref, w_ref, o_ref, wbf_ref, *, n_p):
    i = pl.program_id(1)

    @pl.when(i == 0)
    def _():
        wbf_ref[...] = w_ref[...].astype(BF16)

    def run(x_ref):
        o_ref[...] = jnp.dot(x_ref[...], wbf_ref[...], preferred_element_type=F32).astype(o_ref.dtype)

    @pl.when(i < n_p)
    def _():
        run(xp_ref)

    @pl.when(i >= n_p)
    def _():
        run(xs_ref)


def _split_rows(act, mp):
    if isinstance(act, tuple):
        return act[0], act[1], 0
    return act, act, mp


def matmul(act, w, mp, ms, out_dtype=F32, tn=512, tm_cap=512):
    xp, xs, s_off = _split_rows(act, mp)
    k, n = w.shape
    tn = min(tn, n)
    assert n % tn == 0
    tm = _row_tile(mp, ms, tm_cap)
    n_p, n_s = mp // tm, ms // tm
    s_blk0 = s_off // tm
    return pl.pallas_call(
        functools.partial(_mm_body, n_p=n_p),
        grid=(n // tn, n_p + n_s),
        in_specs=[
            pl.BlockSpec((tm, k), lambda j, i: (jnp.minimum(i, n_p - 1), 0)),
            pl.BlockSpec((tm, k), lambda j, i: (s_blk0 + jnp.maximum(i - n_p, 0), 0)),
            pl.BlockSpec((k, tn), lambda j, i: (0, j)),
        ],
        out_specs=pl.BlockSpec((tm, tn), lambda j, i: (i, j)),
        out_shape=jax.ShapeDtypeStruct((mp + ms, n), out_dtype),
        scratch_shapes=[pltpu.VMEM((k, tn), BF16)],
        compiler_params=_params("arbitrary", "arbitrary"),
    )(xp, xs, w)


def _lora_body(xp_ref, xs_ref, w1_ref, w2_ref, o_ref, w1bf_ref, w2bf_ref, *, n_p, act):
    i = pl.program_id(0)

    @pl.when(i == 0)
    def _():
        w1bf_ref[...] = w1_ref[...].astype(BF16)
        w2bf_ref[...] = w2_ref[...].astype(BF16)

    def run(x_ref):
        t = jnp.dot(x_ref[...], w1bf_ref[...], preferred_element_type=F32)
        if act == "tanh":
            t = jnp.tanh(t)
        elif act == "sigmoid":
            t = _sigmoid(t)
        o_ref[...] = jnp.dot(t.astype(BF16), w2bf_ref[...], preferred_element_type=F32)

    @pl.when(i < n_p)
    def _():
        run(xp_ref)

    @pl.when(i >= n_p)
    def _():
        run(xs_ref)


def lora(act_in, w1, w2, mp, ms, act, tm_cap=256):
    xp, xs, s_off = _split_rows(act_in, mp)
    pad = -w1.shape[1] % LANES
    w1 = jnp.pad(w1, ((0, 0), (0, pad)))
    w2 = jnp.pad(w2, ((0, pad), (0, 0)))
    k, r = w1.shape
    n = w2.shape[1]
    tm = _row_tile(mp, ms, tm_cap)
    n_p, n_s = mp // tm, ms // tm
    s_blk0 = s_off // tm
    return pl.pallas_call(
        functools.partial(_lora_body, n_p=n_p, act=act),
        grid=(n_p + n_s,),
        in_specs=[
            pl.BlockSpec((tm, k), lambda i: (jnp.minimum(i, n_p - 1), 0)),
            pl.BlockSpec((tm, k), lambda i: (s_blk0 + jnp.maximum(i - n_p, 0), 0)),
            pl.BlockSpec((k, r), lambda i: (0, 0)),
            pl.BlockSpec((r, n), lambda i: (0, 0)),
        ],
        out_specs=pl.BlockSpec((tm, n), lambda i: (i, 0)),
        out_shape=jax.ShapeDtypeStruct((mp + ms, n), F32),
        scratch_shapes=[pltpu.VMEM((k, r), BF16), pltpu.VMEM((r, n), BF16)],
        compiler_params=_params("arbitrary"),
    )(xp, xs, w1, w2)


def _layer_norm_rows(z, g, b):
    mu = jnp.mean(z, axis=-1, keepdims=True)
    zc = z - mu
    var = jnp.mean(zc * zc, axis=-1, keepdims=True)
    return zc * lax.rsqrt(var + LN_EPS) * g + b


def _ln_body(x_ref, mix_ref, g_ref, b_ref, *rest, alpha, with_router):
    if with_router:
        wr_ref, br_ref, o_ref, obf_ref, lg_ref = rest
    else:
        o_ref, obf_ref = rest
    y = _layer_norm_rows(alpha * x_ref[...] + mix_ref[...], g_ref[...], b_ref[...])
    o_ref[...] = y
    obf_ref[...] = y.astype(BF16)
    if with_router:
        lg_ref[...] = jnp.dot(y, wr_ref[...], preferred_element_type=F32,
                              precision=lax.Precision.HIGHEST) + br_ref[...]


def residual_layer_norm(x, mix, g, b, alpha, router=None, tm=256):
    m, d = x.shape
    tm = _div_tile(m, tm)
    row = pl.BlockSpec((tm, d), lambda i: (i, 0))
    vec = pl.BlockSpec((1, d), lambda i: (0, 0))
    in_specs = [row, row, vec, vec]
    args = [x, mix, g.reshape(1, d), b.reshape(1, d)]
    out_specs = [row, row]
    out_shape = [jax.ShapeDtypeStruct((m, d), F32), jax.ShapeDtypeStruct((m, d), BF16)]
    if router is not None:
        wr, br = router
        in_specs += [pl.BlockSpec((d, ROUTE_PAD), lambda i: (0, 0)), pl.BlockSpec((1, ROUTE_PAD), lambda i: (0, 0))]
        args += [wr, br]
        out_specs.append(pl.BlockSpec((tm, ROUTE_PAD), lambda i: (i, 0)))
        out_shape.append(jax.ShapeDtypeStruct((m, ROUTE_PAD), F32))
    return pl.pallas_call(
        functools.partial(_ln_body, alpha=alpha, with_router=router is not None),
        grid=(m // tm,),
        in_specs=in_specs,
        out_specs=out_specs,
        out_shape=out_shape,
        compiler_params=_params("arbitrary"),
    )(*args)


def _lru_gates(xc, gate, wa_ref, wx_ref, ba_ref, bx_ref, lam_ref):
    xcb = xc.astype(BF16)
    r = _sigmoid(jnp.dot(xcb, wa_ref[0].astype(BF16), preferred_element_type=F32) + ba_ref[...])
    i = _sigmoid(jnp.dot(xcb, wx_ref[0].astype(BF16), preferred_element_type=F32) + bx_ref[...])
    log_a = -LRU_C * r * _softplus(-lam_ref[...])
    a = jnp.exp(log_a)
    mult = jnp.sqrt(1.0 - a * a)
    return a, mult * (i * xc), _gelu_tanh(gate)


def _lru_prompt_body(xb_ref, gate_ref, cw_ref, cb_ref, wa_ref, wx_ref, ba_ref, bx_ref, lam_ref,
                     y_ref, tail_ref, h_ref, prev_ref, hc_ref):
    c = pl.program_id(2)

    @pl.when(c == 0)
    def _():
        prev_ref[...] = jnp.zeros_like(prev_ref)
        hc_ref[...] = jnp.zeros_like(hc_ref)

    xb = xb_ref[...]
    tt = xb.shape[0]
    prev8 = prev_ref[...]
    row8 = lax.broadcasted_iota(jnp.int32, prev8.shape, 0)
    cw = cw_ref[...]
    xc = cb_ref[...] + xb * cw[CONV_W - 1:CONV_W]
    for j in range(1, CONV_W):
        rolled = pltpu.roll(xb, j, 0)
        head = jnp.where(row8 < j, pltpu.roll(prev8, j, 0), rolled[:SUBLANES])
        xj = jnp.concatenate([head, rolled[SUBLANES:]], axis=0)
        xc = xc + xj * cw[CONV_W - 1 - j:CONV_W - j]

    a, b, gg = _lru_gates(xc, gate_ref[...], wa_ref, wx_ref, ba_ref, bx_ref, lam_ref)

    row = lax.broadcasted_iota(jnp.int32, a.shape, 0)
    s = 1
    while s < tt:
        keep = row >= s
        b = jnp.where(keep, a * pltpu.roll(b, s, 0) + b, b)
        a = jnp.where(keep, a * pltpu.roll(a, s, 0), a)
        s *= 2
    h = a * hc_ref[...] + b

    y_ref[...] = (h * gg).astype(y_ref.dtype)
    tail_ref[0] = xb[tt - SUBLANES:]
    h_ref[0] = h[tt - 1:]
    prev_ref[...] = xb[tt - SUBLANES:]
    hc_ref[...] = h[tt - 1:]


def lru_prompt(proj, p, j, bsz, t_len, tt=256):
    d = proj.shape[1] // 2
    nb, bw = p['lru_w_a'].shape[1], p['lru_w_a'].shape[2]
    tt = min(tt, t_len)
    assert t_len % tt == 0 and tt % SUBLANES == 0
    nc = t_len // tt
    vec = lambda a: a[j].reshape(1, d)
    vspec = pl.BlockSpec((1, bw), lambda b, n, c: (0, n))
    wspec = pl.BlockSpec((1, bw, bw), lambda b, n, c: (n, 0, 0))
    return pl.pallas_call(
        _lru_prompt_body,
        grid=(bsz, nb, nc),
        in_specs=[
            pl.BlockSpec((tt, bw), lambda b, n, c: (b * nc + c, n)),
            pl.BlockSpec((tt, bw), lambda b, n, c: (b * nc + c, nb + n)),
            pl.BlockSpec((CONV_W, bw), lambda b, n, c: (0, n)),
            vspec, wspec, wspec, vspec, vspec, vspec,
        ],
        out_specs=[
            pl.BlockSpec((tt, bw), lambda b, n, c: (b * nc + c, n)),
            pl.BlockSpec((1, SUBLANES, bw), lambda b, n, c: (b, 0, n)),
            pl.BlockSpec((1, 1, bw), lambda b, n, c: (b, 0, n)),
        ],
        out_shape=[
            jax.ShapeDtypeStruct((bsz * t_len, d), BF16),
            jax.ShapeDtypeStruct((bsz, SUBLANES, d), F32),
            jax.ShapeDtypeStruct((bsz, 1, d), F32),
        ],
        scratch_shapes=[pltpu.VMEM((SUBLANES, bw), F32), pltpu.VMEM((1, bw), F32)],
        compiler_params=_params("arbitrary", "arbitrary", "arbitrary"),
    )(proj, proj, p['lru_conv_w'][j], vec(p['lru_conv_b']), p['lru_w_a'][j], p['lru_w_x'][j],
      vec(p['lru_b_a']), vec(p['lru_b_x']), vec(p['lru_lam']))


def _lru_sample_body(xb_ref, gate_ref, cst_ref, h0_ref, cw_ref, cb_ref, wa_ref, wx_ref, ba_ref, bx_ref,
                     lam_ref, y_ref, tail_ref, h_ref, *, t_len, nseq):
    cw = cw_ref[...]
    xs = [cst_ref[q] for q in range(CONV_W - 1)]
    xs += [xb_ref[t * nseq:(t + 1) * nseq] for t in range(t_len)]
    xcs = []
    for t in range(t_len):
        xc = cb_ref[...] + xs[t] * cw[0:1]
        for q in range(1, CONV_W):
            xc = xc + xs[t + q] * cw[q:q + 1]
        xcs.append(xc)
    a, b, gg = _lru_gates(jnp.concatenate(xcs, axis=0), gate_ref[...], wa_ref, wx_ref, ba_ref, bx_ref, lam_ref)
    h = h0_ref[...]
    for t in range(t_len):
        sl = slice(t * nseq, (t + 1) * nseq)
        h = a[sl] * h + b[sl]
        y_ref[sl, :] = (h * gg[sl]).astype(y_ref.dtype)
    h_ref[...] = h
    for q in range(CONV_W - 1):
        tail_ref[q] = xs[t_len + q]


def lru_sample(proj, conv_st, h0, p, j, mp, nseq, t_len):
    d = proj.shape[1] // 2
    nb, bw = p['lru_w_a'].shape[1], p['lru_w_a'].shape[2]
    ms = nseq * t_len
    assert mp % ms == 0
    blk0 = mp // ms
    vec = lambda a: a[j].reshape(1, d)
    vspec = pl.BlockSpec((1, bw), lambda n: (0, n))
    wspec = pl.BlockSpec((1, bw, bw), lambda n: (n, 0, 0))
    return pl.pallas_call(
        functools.partial(_lru_sample_body, t_len=t_len, nseq=nseq),
        grid=(nb,),
        in_specs=[
            pl.BlockSpec((ms, bw), lambda n: (blk0, n)),
            pl.BlockSpec((ms, bw), lambda n: (blk0, nb + n)),
            pl.BlockSpec((CONV_W - 1, nseq, bw), lambda n: (0, 0, n)),
            pl.BlockSpec((nseq, bw), lambda n: (0, n)),
            pl.BlockSpec((CONV_W, bw), lambda n: (0, n)),
            vspec, wspec, wspec, vspec, vspec, vspec,
        ],
        out_specs=[
            pl.BlockSpec((ms, bw), lambda n: (0, n)),
            pl.BlockSpec((CONV_W - 1, nseq, bw), lambda n: (0, 0, n)),
            pl.BlockSpec((nseq, bw), lambda n: (0, n)),
        ],
        out_shape=[
            jax.ShapeDtypeStruct((ms, d), BF16),
            jax.ShapeDtypeStruct((CONV_W - 1, nseq, d), F32),
            jax.ShapeDtypeStruct((nseq, d), F32),
        ],
        compiler_params=_params("arbitrary"),
    )(proj, proj, conv_st, h0, p['lru_conv_w'][j], vec(p['lru_conv_b']), p['lru_w_a'][j], p['lru_w_x'][j],
      vec(p['lru_b_a']), vec(p['lru_b_x']), vec(p['lru_lam']))


def _mix_store(x, prev, mu_ref, out_refs):
    xx = prev - x
    for q, o_ref in enumerate(out_refs):
        o_ref[...] = (x + xx * mu_ref[q:q + 1]).astype(o_ref.dtype)


def _shift_prompt_body(x_ref, mu_ref, *rest, blocks_per_seq):
    out_refs, carry_ref = rest[:-1], rest[-1]
    i = pl.program_id(1)
    x = x_ref[...]
    first = jnp.where(i % blocks_per_seq == 0, jnp.zeros_like(carry_ref[...]), carry_ref[...])
    row = lax.broadcasted_iota(jnp.int32, x.shape, 0)
    prev = jnp.where(row == 0, first, pltpu.roll(x, 1, 0))
    _mix_store(x, prev, mu_ref, out_refs)
    carry_ref[...] = x[x.shape[0] - 1:]


def shift_prompt(x, mu, mp, t_len, tm=256, tc=1024):
    d = x.shape[1]
    tm, tc = min(tm, t_len), min(tc, d)
    assert t_len % tm == 0 and d % tc == 0
    nq = mu.shape[0]
    spec = pl.BlockSpec((tm, tc), lambda c, i: (i, c))
    return pl.pallas_call(
        functools.partial(_shift_prompt_body, blocks_per_seq=t_len // tm),
        grid=(d // tc, mp // tm),
        in_specs=[spec, pl.BlockSpec((nq, tc), lambda c, i: (0, c))],
        out_specs=[spec] * nq,
        out_shape=[jax.ShapeDtypeStruct((mp, d), BF16)] * nq,
        scratch_shapes=[pltpu.VMEM((1, tc), F32)],
        compiler_params=_params("arbitrary", "arbitrary"),
    )(x, mu)


def _shift_sample_body(x_ref, st_ref, mu_ref, *out_refs, nseq):
    x = x_ref[...]
    prev = jnp.concatenate([st_ref[...], x[:x.shape[0] - nseq]], axis=0)
    _mix_store(x, prev, mu_ref, out_refs)


def shift_sample(x, state, mu, mp, nseq, t_len, tc=1024):
    d = x.shape[1]
    tc = min(tc, d)
    ms = nseq * t_len
    blk0 = mp // ms
    nq = mu.shape[0]
    return pl.pallas_call(
        functools.partial(_shift_sample_body, nseq=nseq),
        grid=(d // tc,),
        in_specs=[pl.BlockSpec((ms, tc), lambda c: (blk0, c)),
                  pl.BlockSpec((nseq, tc), lambda c: (0, c)),
                  pl.BlockSpec((nq, tc), lambda c: (0, c))],
        out_specs=[pl.BlockSpec((ms, tc), lambda c: (0, c))] * nq,
        out_shape=[jax.ShapeDtypeStruct((ms, d), BF16)] * nq,
        compiler_params=_params("arbitrary"),
    )(x, state, mu)


def _seg_sum(x):
    lo = lax.broadcasted_iota(jnp.int32, x.shape, x.ndim - 1) < RWKV_HEAD
    s_lo = jnp.sum(jnp.where(lo, x, 0.0), axis=-1, keepdims=True)
    s_hi = jnp.sum(jnp.where(lo, 0.0, x), axis=-1, keepdims=True)
    return jnp.where(lo, s_lo, s_hi)


def _wkv_body(*refs, nb, nt, n_pairs, has_state, has_vres, chunked):
    it = iter(refs)
    r_ref, k_ref, v_ref, wr_ref, ar_ref, g_ref = (next(it) for _ in range(6))
    if has_vres:
        vr_ref, vf_ref, v0_ref = (next(it) for _ in range(3))
    w0_ref, a0_ref, kk_ref, ka_ref, rk_ref, lg_ref, lb_ref = (next(it) for _ in range(7))
    if has_state:
        s0_ref = next(it)
    y_ref, s_ref = next(it), next(it)
    dec_s, kkn_s, kab_s, k2_s, v_s, ys_s = (next(it) for _ in range(6))

    if chunked:
        @pl.when(pl.program_id(2) == 0)
        def _():
            s_ref[...] = jnp.zeros_like(s_ref)
    else:
        s_ref[...] = s0_ref[...] if has_state else jnp.zeros_like(s_ref)

    r = r_ref[...]
    k = k_ref[...]
    v = v_ref[...]
    w = -_softplus(-(w0_ref[...] + wr_ref[...])) - 0.5
    a = _sigmoid(a0_ref[...] + ar_ref[...])
    if has_vres:
        v = v + (vf_ref[...] - v) * _sigmoid(v0_ref[...] + vr_ref[...])
    k2 = k * (1.0 + (a - 1.0) * ka_ref[...])
    bonus_terms = r * k2 * rk_ref[...]
    for pr in range(n_pairs):
        ls = slice(pr * PAIR, (pr + 1) * PAIR)
        kkr = k[:, ls] * kk_ref[:, ls]
        kk = kkr / jnp.maximum(jnp.sqrt(_seg_sum(kkr * kkr)), 1e-12)
        kkn_s[:, ls] = -kk
        kab_s[:, ls] = kk * a[:, ls]
    dec_s[...] = jnp.exp(-jnp.exp(w))
    k2_s[...] = k2
    v_s[...] = v

    eye = (lax.broadcasted_iota(jnp.int32, (RWKV_HEAD, PAIR), 1) % RWKV_HEAD
           == lax.broadcasted_iota(jnp.int32, (RWKV_HEAD, PAIR), 0))

    sub8 = lax.broadcasted_iota(jnp.int32, (SUBLANES, PAIR), 0)
    operand_refs = (kkn_s, v_s, dec_s, kab_s, k2_s, r_ref)

    def load_tiles(base, ls):
        rows = pl.ds(pl.multiple_of(base, SUBLANES), SUBLANES)
        return tuple(ref[rows, ls] for ref in operand_refs)

    def step(s, tiles, i, ytile):
        kkn, vv, dec, kab, k2r, rr = (t[i:i + 1] for t in tiles)
        sa = _seg_sum(s * kkn)
        vcol = _seg_sum(jnp.where(eye, vv, 0.0))
        s = s * dec + sa * kab + vcol * k2r
        ycol = _seg_sum(s * rr)
        yrow = jnp.sum(jnp.where(eye, ycol, 0.0), axis=0, keepdims=True)
        return s, jnp.where(sub8 == i, yrow, ytile)

    if nb == 1:
        def time_group(gi, states):
            base = gi * SUBLANES
            states = list(states)
            for pr in range(n_pairs):
                ls = slice(pr * PAIR, (pr + 1) * PAIR)
                tiles = load_tiles(base, ls)
                ytile = jnp.zeros((SUBLANES, PAIR), F32)
                for i in range(SUBLANES):
                    states[pr], ytile = step(states[pr], tiles, i, ytile)
                ys_s[pl.ds(pl.multiple_of(base, SUBLANES), SUBLANES), ls] = ytile
            return tuple(states)

        states = lax.fori_loop(0, nt // SUBLANES, time_group, tuple(s_ref[0, pr] for pr in range(n_pairs)))
        for pr in range(n_pairs):
            s_ref[0, pr] = states[pr]
    else:
        for t in range(nt):
            def seq_group(gi, carry):
                b0 = gi * SUBLANES
                for pr in range(n_pairs):
                    ls = slice(pr * PAIR, (pr + 1) * PAIR)
                    tiles = load_tiles(t * nb + b0, ls)
                    states = [s_ref[b0 + u, pr] for u in range(SUBLANES)]
                    ytile = jnp.zeros((SUBLANES, PAIR), F32)
                    for u in range(SUBLANES):
                        states[u], ytile = step(states[u], tiles, u, ytile)
                    for u in range(SUBLANES):
                        s_ref[b0 + u, pr] = states[u]
                    ys_s[pl.ds(pl.multiple_of(t * nb + b0, SUBLANES), SUBLANES), ls] = ytile
                return carry

            lax.fori_loop(0, nb // SUBLANES, seq_group, 0)

    for pr in range(n_pairs):
        ls = slice(pr * PAIR, (pr + 1) * PAIR)
        y = ys_s[:, ls]
        mu = _seg_sum(y) * (1.0 / RWKV_HEAD)
        yc = y - mu
        var = _seg_sum(yc * yc) * (1.0 / RWKV_HEAD)
        yn = yc * lax.rsqrt(var + GN_EPS) * lg_ref[:, ls] + lb_ref[:, ls]
        bonus = _seg_sum(bonus_terms[:, ls]) * v[:, ls]
        y_ref[:, ls] = ((yn + bonus) * g_ref[:, ls]).astype(y_ref.dtype)


def wkv(acts, vres, p, j, state, *, row_blk0, nb, nt, n_seq_blocks, n_chunks, n_pairs):
    d = acts[0].shape[1]
    lw = n_pairs * PAIR
    n_lane_blocks = d // lw
    rows = nb * nt
    chunked = n_chunks > 1
    assert not (chunked and state is not None)
    has_vres = vres is not None
    if chunked:
        grid = (n_seq_blocks, n_lane_blocks, n_chunks)
        act_map = lambda b, l, c: (row_blk0 + b * n_chunks + c, l)
        out_map = lambda b, l, c: (b * n_chunks + c, l)
        vec_map = lambda b, l, c: (0, l)
        st_map = lambda b, l, c: (b, l, 0, 0)
        sem = ("arbitrary", "arbitrary", "arbitrary")
    else:
        grid = (n_seq_blocks, n_lane_blocks)
        act_map = lambda b, l: (row_blk0 + b, l)
        out_map = lambda b, l: (b, l)
        vec_map = lambda b, l: (0, l)
        st_map = lambda b, l: (b, l, 0, 0)
        sem = ("arbitrary", "arbitrary")
    aspec = pl.BlockSpec((rows, lw), act_map)
    vspec = pl.BlockSpec((1, lw), vec_map)
    sspec = pl.BlockSpec((nb, n_pairs, RWKV_HEAD, PAIR), st_map)
    vec = lambda a: a.reshape(1, d)
    args, in_specs = list(acts), [aspec] * 6
    if has_vres:
        v_raw, v_first, v0 = vres
        args += [v_raw, v_first, vec(v0)]
        in_specs += [aspec, aspec, vspec]
    args += [vec(p['rw_w0'][j]), vec(p['rw_a0'][j]), vec(p['rw_k_k'][j]), vec(p['rw_k_a'][j]),
             vec(p['rw_r_k'][j]), vec(p['rw_lnx_g'][j]), vec(p['rw_lnx_b'][j])]
    in_specs += [vspec] * 7
    if state is not None:
        args.append(state)
        in_specs.append(sspec)
    n_seq = n_seq_blocks * nb
    return pl.pallas_call(
        functools.partial(_wkv_body, nb=nb, nt=nt, n_pairs=n_pairs,
                          has_state=state is not None, has_vres=has_vres, chunked=chunked),
        grid=grid,
        in_specs=in_specs,
        out_specs=[pl.BlockSpec((rows, lw), out_map), sspec],
        out_shape=[jax.ShapeDtypeStruct((n_seq_blocks * n_chunks * rows, d), BF16),
                   jax.ShapeDtypeStruct((n_seq, d // PAIR, RWKV_HEAD, PAIR), F32)],
        scratch_shapes=[pltpu.VMEM((rows, lw), F32)] * 6,
        compiler_params=_params(*sem),
    )(*args)


def _pair_state(s):
    b, h, nv, nk = s.shape
    return s.reshape(b, h // 2, 2, nv, nk).transpose(0, 1, 3, 2, 4).reshape(b, h // 2, nv, 2 * nk)


def _unpair_state(s):
    b, hp, nv, nk2 = s.shape
    return s.reshape(b, hp, nv, 2, nk2 // 2).transpose(0, 1, 3, 2, 4).reshape(b, hp * 2, nv, nk2 // 2)


def _route(logits, n_groups, n_experts, tm):
    m = logits.shape[0]
    epg = n_experts // n_groups
    g_logits = logits[:, :n_groups]
    g_idx = jnp.argmax(g_logits, axis=-1)
    p_group = jnp.max(jax.nn.softmax(g_logits, axis=-1), axis=-1, keepdims=True)
    e_logits = logits[:, n_groups:n_groups + n_experts].reshape(m, n_groups, epg)
    e_in = jnp.take_along_axis(e_logits, g_idx[:, None, None], axis=1)[:, 0]
    top_val, top_idx = lax.top_k(e_in, TOP_K)
    gate = p_group * jax.nn.softmax(top_val, axis=-1)
    expert_flat = (g_idx[:, None] * epg + top_idx).astype(jnp.int32).reshape(-1)

    n_assign = m * TOP_K
    order = jnp.argsort(expert_flat)
    e_sorted = expert_flat[order]
    tok_sorted = (order // TOP_K).astype(jnp.int32)
    counts = jnp.bincount(expert_flat, length=n_experts).astype(jnp.int32)
    start = jnp.cumsum(counts) - counts
    nblk_e = (counts + tm - 1) // tm
    blk_end = jnp.cumsum(nblk_e)
    blk_start = blk_end - nblk_e
    slot_sorted = blk_start[e_sorted] * tm + (jnp.arange(n_assign, dtype=jnp.int32) - start[e_sorted])
    n_blocks = (n_assign + n_experts * (tm - 1) + tm - 1) // tm
    slot_tok = jnp.zeros((n_blocks * tm,), jnp.int32).at[slot_sorted].set(tok_sorted)
    slot_of_assign = jnp.zeros((n_assign,), jnp.int32).at[order].set(slot_sorted)
    nblk_e = nblk_e.at[n_experts - 1].add(n_blocks - blk_end[-1])
    return slot_tok, slot_of_assign.reshape(m, TOP_K), gate.astype(F32), blk_start, nblk_e, n_blocks


def _items(blk_start, nblk_e, n_blocks, n_chunks):
    n_items = n_blocks * n_chunks
    it = jnp.arange(n_items, dtype=jnp.int32)
    item_end = jnp.cumsum(nblk_e * n_chunks)
    e = jnp.minimum(jnp.searchsorted(item_end, it, side='right'), nblk_e.shape[0] - 1).astype(jnp.int32)
    local = it - (item_end[e] - nblk_e[e] * n_chunks)
    nb_e = jnp.maximum(nblk_e[e], 1)
    return e, (local // nb_e).astype(jnp.int32), (blk_start[e] + local % nb_e).astype(jnp.int32)


def _gather_body(tok_ref, x_hbm, o_ref, buf_ref, sem, *, tm):
    base = pl.program_id(0) * tm

    def copy(r):
        return pltpu.make_async_copy(x_hbm.at[pl.ds(tok_ref[base + r], 1)], buf_ref.at[pl.ds(r, 1)], sem)

    def start(r, c):
        copy(r).start()
        return c

    def wait(r, c):
        copy(r).wait()
        return c

    lax.fori_loop(0, tm, start, 0)
    lax.fori_loop(0, tm, wait, 0)
    o_ref[...] = buf_ref[...].astype(o_ref.dtype)


def gather_rows(x, slot_tok, tm):
    d = x.shape[1]
    n_slots = slot_tok.shape[0]
    return pl.pallas_call(
        functools.partial(_gather_body, tm=tm),
        grid_spec=pltpu.PrefetchScalarGridSpec(
            num_scalar_prefetch=1,
            grid=(n_slots // tm,),
            in_specs=[pl.BlockSpec(memory_space=pl.ANY)],
            out_specs=pl.BlockSpec((tm, d), lambda i, tok: (i, 0)),
            scratch_shapes=[pltpu.VMEM((tm, d), F32), pltpu.SemaphoreType.DMA(())],
        ),
        out_shape=jax.ShapeDtypeStruct((n_slots, d), BF16),
        compiler_params=_params("arbitrary"),
    )(slot_tok, x)


def _ffn_up_body(e_ref, j_ref, b_ref, x_ref, wg_ref, wu_ref, o_ref, wgbf_ref, wubf_ref):
    it = pl.program_id(0)
    prev = jnp.maximum(it - 1, 0)
    fresh = (it == 0) | (e_ref[it] != e_ref[prev]) | (j_ref[it] != j_ref[prev])

    @pl.when(fresh)
    def _():
        wgbf_ref[...] = wg_ref[0].astype(BF16)
        wubf_ref[...] = wu_ref[0].astype(BF16)

    x = x_ref[...]
    hg = jnp.dot(x, wgbf_ref[...], preferred_element_type=F32)
    hu = jnp.dot(x, wubf_ref[...], preferred_element_type=F32)
    o_ref[...] = (hg * _sigmoid(hg) * hu).astype(o_ref.dtype)


def _ffn_down_body(e_ref, j_ref, b_ref, h_ref, wd_ref, o_ref, wdbf_ref):
    it = pl.program_id(0)
    prev = jnp.maximum(it - 1, 0)
    fresh = (it == 0) | (e_ref[it] != e_ref[prev]) | (j_ref[it] != j_ref[prev])

    @pl.when(fresh)
    def _():
        wdbf_ref[...] = wd_ref[0].astype(BF16)

    o_ref[...] = jnp.dot(h_ref[...], wdbf_ref[...], preferred_element_type=F32)


def expert_ffn(x_sorted, blk_start, nblk_e, n_blocks, w_gate, w_up, w_down, tm, tn_up=512, tn_down=2048):
    n_slots, d = x_sorted.shape
    de = w_gate.shape[2]
    tn_up, tn_down = min(tn_up, de), min(tn_down, d)
    e1, j1, b1 = _items(blk_start, nblk_e, n_blocks, de // tn_up)
    hid = pl.pallas_call(
        _ffn_up_body,
        grid_spec=pltpu.PrefetchScalarGridSpec(
            num_scalar_prefetch=3,
            grid=(e1.shape[0],),
            in_specs=[pl.BlockSpec((tm, d), lambda it, e, j, b: (b[it], 0)),
                      pl.BlockSpec((1, d, tn_up), lambda it, e, j, b: (e[it], 0, j[it])),
                      pl.BlockSpec((1, d, tn_up), lambda it, e, j, b: (e[it], 0, j[it]))],
            out_specs=pl.BlockSpec((tm, tn_up), lambda it, e, j, b: (b[it], j[it])),
            scratch_shapes=[pltpu.VMEM((d, tn_up), BF16), pltpu.VMEM((d, tn_up), BF16)],
        ),
        out_shape=jax.ShapeDtypeStruct((n_slots, de), BF16),
        compiler_params=_params("arbitrary"),
    )(e1, j1, b1, x_sorted, w_gate, w_up)
    e2, j2, b2 = _items(blk_start, nblk_e, n_blocks, d // tn_down)
    return pl.pallas_call(
        _ffn_down_body,
        grid_spec=pltpu.PrefetchScalarGridSpec(
            num_scalar_prefetch=3,
            grid=(e2.shape[0],),
            in_specs=[pl.BlockSpec((tm, de), lambda it, e, j, b: (b[it], 0)),
                      pl.BlockSpec((1, de, tn_down), lambda it, e, j, b: (e[it], 0, j[it]))],
            out_specs=pl.BlockSpec((tm, tn_down), lambda it, e, j, b: (b[it], j[it])),
            scratch_shapes=[pltpu.VMEM((de, tn_down), BF16)],
        ),
        out_shape=jax.ShapeDtypeStruct((n_slots, d), F32),
        compiler_params=_params("arbitrary"),
    )(e2, j2, b2, hid, w_down)


def _combine_ln_body(slot_ref, x_ref, gate_ref, g_ref, b_ref, y_hbm, o_ref, obf_ref, buf_ref, sem, *, tm, alpha):
    base = pl.program_id(0) * tm

    def copy(q):
        return pltpu.make_async_copy(y_hbm.at[pl.ds(slot_ref[base * TOP_K + q], 1)],
                                     buf_ref.at[q % TOP_K, pl.ds(q // TOP_K, 1)], sem)

    def start(q, c):
        copy(q).start()
        return c

    def wait(q, c):
        copy(q).wait()
        return c

    lax.fori_loop(0, tm * TOP_K, start, 0)
    lax.fori_loop(0, tm * TOP_K, wait, 0)
    gate = gate_ref[...]
    ffn = buf_ref[0] * gate[:, 0:1]
    for q in range(1, TOP_K):
        ffn = ffn + buf_ref[q] * gate[:, q:q + 1]
    y = _layer_norm_rows(alpha * x_ref[...] + ffn, g_ref[...], b_ref[...])
    o_ref[...] = y
    obf_ref[...] = y.astype(BF16)


def combine_layer_norm(x, out_slots, slot_of_assign, gate, g, b, alpha, tm=256):
    m, d = x.shape
    tm = _div_tile(m, tm)
    row = pl.BlockSpec((tm, d), lambda i, s: (i, 0))
    vec = pl.BlockSpec((1, d), lambda i, s: (0, 0))
    return pl.pallas_call(
        functools.partial(_combine_ln_body, tm=tm, alpha=alpha),
        grid_spec=pltpu.PrefetchScalarGridSpec(
            num_scalar_prefetch=1,
            grid=(m // tm,),
            in_specs=[row, pl.BlockSpec((tm, TOP_K), lambda i, s: (i, 0)), vec, vec,
                      pl.BlockSpec(memory_space=pl.ANY)],
            out_specs=[row, row],
            scratch_shapes=[pltpu.VMEM((TOP_K, tm, d), F32), pltpu.SemaphoreType.DMA(())],
        ),
        out_shape=[jax.ShapeDtypeStruct((m, d), F32), jax.ShapeDtypeStruct((m, d), BF16)],
        compiler_params=_params("arbitrary"),
    )(slot_of_assign.reshape(-1), x, gate, g.reshape(1, d), b.reshape(1, d), out_slots)


def _trunk(x_prompt, x_sample, state_conv, state_h, state_shift, state_wkv, p, moe_tm=128):
    bsz, t_len, d = x_prompt.shape
    nseq, dec_t, _ = x_sample.shape
    mp, ms = bsz * t_len, nseq * dec_t
    depth = p['ln_g'].shape[0]
    alpha = (2 * depth) ** 0.25
    n_groups = p['moe_w_group'].shape[2]
    n_experts = p['moe_w_expert'].shape[2]

    x = jnp.concatenate([x_prompt.reshape(mp, d), x_sample.transpose(1, 0, 2).reshape(ms, d)], axis=0)
    xbf = x.astype(BF16)
    conv_s_tm = state_conv.transpose(0, 2, 1, 3)

    conv_p, h_p, shift_p, wkv_p, conv_s, h_s, shift_s, wkv_s = ([] for _ in range(8))
    v_first = None
    for i in range(depth):
        j = i // 2
        if i % 2 == 0:
            proj = matmul(xbf, p['lru_w_in'][j], mp, ms)
            y_p, tail_p, hl_p = lru_prompt(proj, p, j, bsz, t_len)
            y_s, tail_s, hl_s = lru_sample(proj, conv_s_tm[j], state_h[j], p, j, mp, nseq, dec_t)
            mix = matmul((y_p, y_s), p['lru_w_out'][j], mp, ms)
            conv_p.append(tail_p[:, SUBLANES - (CONV_W - 1):])
            h_p.append(hl_p[:, 0])
            conv_s.append(tail_s.transpose(1, 0, 2))
            h_s.append(hl_s)
        else:
            mu = p['rw_mu'][j]
            mixed_p = shift_prompt(x, mu, mp, t_len)
            mixed_s = shift_sample(x, state_shift[j], mu, mp, nseq, dec_t)
            xr, xw, xk, xv, xa, xg = zip(mixed_p, mixed_s)
            r = matmul(xr, p['rw_w_r'][j], mp, ms)
            k = matmul(xk, p['rw_w_k'][j], mp, ms)
            v = matmul(xv, p['rw_w_v'][j], mp, ms)
            w_raw = lora(xw, p['rw_w1'][j], p['rw_w2'][j], mp, ms, "tanh")
            a_raw = lora(xa, p['rw_a1'][j], p['rw_a2'][j], mp, ms, "none")
            g = lora(xg, p['rw_g1'][j], p['rw_g2'][j], mp, ms, "sigmoid")
            if j == 0:
                vres, v_first = None, v
            else:
                v_raw = lora(xv, p['rw_v1'][j - 1], p['rw_v2'][j - 1], mp, ms, "none")
                vres = (v_raw, v_first, p['rw_v0'][j - 1])
            acts = (r, k, v, w_raw, a_raw, g)
            tc = min(256, t_len)
            n_pairs_p = min(4, d // PAIR)
            yg_p, st_p = wkv(acts, vres, p, j, None, row_blk0=0, nb=1, nt=tc, n_seq_blocks=bsz,
                             n_chunks=t_len // tc, n_pairs=n_pairs_p)
            yg_s, st_s = wkv(acts, vres, p, j, _pair_state(state_wkv[j]), row_blk0=mp // ms, nb=nseq, nt=dec_t,
                             n_seq_blocks=1, n_chunks=1, n_pairs=1)
            mix = matmul((yg_p, yg_s), p['rw_w_o'][j], mp, ms)
            shift_p.append(x[:mp].reshape(bsz, t_len, d)[:, -1])
            shift_s.append(x[mp + (dec_t - 1) * nseq:])
            wkv_p.append(_unpair_state(st_p))
            wkv_s.append(_unpair_state(st_s))

        w_route = jnp.concatenate([p['moe_w_group'][i], p['moe_w_expert'][i]], axis=1)
        b_route = jnp.concatenate([p['moe_b_group'][i], p['moe_b_expert'][i]], axis=0)
        pad = ROUTE_PAD - w_route.shape[1]
        w_route = jnp.pad(w_route, ((0, 0), (0, pad)))
        b_route = jnp.pad(b_route, (0, pad)).reshape(1, ROUTE_PAD)
        x, xbf, logits = residual_layer_norm(x, mix, p['ln_g'][i, 0], p['ln_b'][i, 0], alpha,
                                             router=(w_route, b_route))
        slot_tok, slot_of_assign, gate, blk_start, nblk_e, n_blocks = _route(logits, n_groups, n_experts, moe_tm)
        x_sorted = gather_rows(x, slot_tok, moe_tm)
        out_slots = expert_ffn(x_sorted, blk_start, nblk_e, n_blocks, p['moe_w_gate'][i], p['moe_w_up'][i],
                               p['moe_w_down'][i], moe_tm)
        x, xbf = combine_layer_norm(x, out_slots, slot_of_assign, gate, p['ln_g'][i, 1], p['ln_b'][i, 1], alpha)

    y_prompt = x[:mp].reshape(bsz, t_len, d)
    y_sample = x[mp:].reshape(dec_t, nseq, d).transpose(1, 0, 2)
    return (y_prompt, y_sample, jnp.stack(conv_p), jnp.stack(h_p), jnp.stack(shift_p), jnp.stack(wkv_p),
            jnp.stack(conv_s), jnp.stack(h_s), jnp.stack(shift_s), jnp.stack(wkv_s))


def kernel(x_prompt, x_sample, state_conv, state_h, state_shift, state_wkv, ln_g, ln_b, lru_w_in, lru_conv_w, lru_conv_b, lru_w_a, lru_b_a, lru_w_x, lru_b_x, lru_lam, lru_w_out, rw_mu, rw_w_r, rw_w_k, rw_w_v, rw_w_o, rw_w0, rw_w1, rw_w2, rw_a0, rw_a1, rw_a2, rw_v0, rw_v1, rw_v2, rw_g1, rw_g2, rw_k_k, rw_k_a, rw_r_k, rw_lnx_g, rw_lnx_b, moe_w_group, moe_b_group, moe_w_expert, moe_b_expert, moe_w_gate, moe_w_up, moe_w_down):
    p = dict(ln_g=ln_g, ln_b=ln_b, lru_w_in=lru_w_in, lru_conv_w=lru_conv_w, lru_conv_b=lru_conv_b,
             lru_w_a=lru_w_a, lru_b_a=lru_b_a, lru_w_x=lru_w_x, lru_b_x=lru_b_x, lru_lam=lru_lam,
             lru_w_out=lru_w_out, rw_mu=rw_mu, rw_w_r=rw_w_r, rw_w_k=rw_w_k, rw_w_v=rw_w_v, rw_w_o=rw_w_o,
             rw_w0=rw_w0, rw_w1=rw_w1, rw_w2=rw_w2, rw_a0=rw_a0, rw_a1=rw_a1, rw_a2=rw_a2,
             rw_v0=rw_v0, rw_v1=rw_v1, rw_v2=rw_v2, rw_g1=rw_g1, rw_g2=rw_g2, rw_k_k=rw_k_k, rw_k_a=rw_k_a,
             rw_r_k=rw_r_k, rw_lnx_g=rw_lnx_g, rw_lnx_b=rw_lnx_b, moe_w_group=moe_w_group,
             moe_b_group=moe_b_group, moe_w_expert=moe_w_expert, moe_b_expert=moe_b_expert,
             moe_w_gate=moe_w_gate, moe_w_up=moe_w_up, moe_w_down=moe_w_down)
    return _trunk(x_prompt, x_sample, state_conv, state_h, state_shift, state_wkv, p)
```

```python
import functools

import jax
import jax.numpy as jnp
from jax import lax
from jax.experimental import pallas as pl
from jax.experimental.pallas import tpu as pltpu

F32 = jnp.float32
BF16 = jnp.bfloat16

V7X_VMEM_LIMIT_BYTES = 56 * 1024 * 1024
LANES = 128
SUBLANES = 8
RWKV_HEAD = 64
PAIR = 2 * RWKV_HEAD
CONV_W = 4
LRU_C = 8.0
GN_EPS = 64e-5
LN_EPS = 1e-5
TOP_K = 2
ROUTE_PAD = LANES


def _params(*sem):
    return pltpu.CompilerParams(dimension_semantics=sem, vmem_limit_bytes=V7X_VMEM_LIMIT_BYTES)


def _sigmoid(x):
    return 1.0 / (1.0 + jnp.exp(-x))


def _softplus(x):
    return jnp.maximum(x, 0.0) + jnp.log1p(jnp.exp(-jnp.abs(x)))


def _gelu_tanh(x):
    return 0.5 * x * (1.0 + jnp.tanh(0.7978845608028654 * (x + 0.044715 * (x * x * x))))


def _row_tile(mp, ms, cap):
    tm = min(cap, ms)
    assert mp % tm == 0 and ms % tm == 0 and tm % SUBLANES == 0, (mp, ms, tm)
    return tm


def _div_tile(m, cap):
    tm = min(cap, m) // SUBLANES * SUBLANES
    while m % tm:
        tm -= SUBLANES
    return tm


def _mm_body(xp_ref, xs_ref, w_ref, o_ref, wbf_ref, *, n_p):
    i = pl.program_id(1)

    @pl.when(i == 0)
    def _():
        wbf_ref[...] = w_ref[...].astype(BF16)

    def run(x_ref):
        o_ref[...] = jnp.dot(x_ref[...], wbf_ref[...], preferred_element_type=F32).astype(o_ref.dtype)

    @pl.when(i < n_p)
    def _():
        run(xp_ref)

    @pl.when(i >= n_p)
    def _():
        run(xs_ref)


def _split_rows(act, mp):
    if isinstance(act, tuple):
        return act[0], act[1], 0
    return act, act, mp


def matmul(act, w, layer, mp, ms, out_dtype=F32, tn=512, tm_cap=512):
    xp, xs, s_off = _split_rows(act, mp)
    _, k, n = w.shape
    tn = min(tn, n)
    assert n % tn == 0
    tm = _row_tile(mp, ms, tm_cap)
    n_p, n_s = mp // tm, ms // tm
    s_blk0 = s_off // tm
    return pl.pallas_call(
        functools.partial(_mm_body, n_p=n_p),
        grid=(n // tn, n_p + n_s),
        in_specs=[
            pl.BlockSpec((tm, k), lambda j, i: (jnp.minimum(i, n_p - 1), 0)),
            pl.BlockSpec((tm, k), lambda j, i: (s_blk0 + jnp.maximum(i - n_p, 0), 0)),
            pl.BlockSpec((None, k, tn), lambda j, i: (layer, 0, j)),
        ],
        out_specs=pl.BlockSpec((tm, tn), lambda j, i: (i, j)),
        out_shape=jax.ShapeDtypeStruct((mp + ms, n), out_dtype),
        scratch_shapes=[pltpu.VMEM((k, tn), BF16)],
        compiler_params=_params("arbitrary", "arbitrary"),
    )(xp, xs, w)


def _lora_body(xp_ref, xs_ref, w1_ref, w2_ref, o_ref, w1bf_ref, w2bf_ref, *, n_p, act):
    i = pl.program_id(0)

    @pl.when(i == 0)
    def _():
        w1bf_ref[...] = w1_ref[...].astype(BF16)
        w2bf_ref[...] = w2_ref[...].astype(BF16)

    def run(x_ref):
        t = jnp.dot(x_ref[...], w1bf_ref[...], preferred_element_type=F32)
        if act == "tanh":
            t = jnp.tanh(t)
        elif act == "sigmoid":
            t = _sigmoid(t)
        o_ref[...] = jnp.dot(t.astype(BF16), w2bf_ref[...], preferred_element_type=F32)

    @pl.when(i < n_p)
    def _():
        run(xp_ref)

    @pl.when(i >= n_p)
    def _():
        run(xs_ref)


def lora(act_in, w1, w2, mp, ms, act, tm_cap=256):
    xp, xs, s_off = _split_rows(act_in, mp)
    pad = -w1.shape[1] % LANES
    w1 = jnp.pad(w1, ((0, 0), (0, pad)))
    w2 = jnp.pad(w2, ((0, pad), (0, 0)))
    k, r = w1.shape
    n = w2.shape[1]
    tm = _row_tile(mp, ms, tm_cap)
    n_p, n_s = mp // tm, ms // tm
    s_blk0 = s_off // tm
    return pl.pallas_call(
        functools.partial(_lora_body, n_p=n_p, act=act),
        grid=(n_p + n_s,),
        in_specs=[
            pl.BlockSpec((tm, k), lambda i: (jnp.minimum(i, n_p - 1), 0)),
            pl.BlockSpec((tm, k), lambda i: (s_blk0 + jnp.maximum(i - n_p, 0), 0)),
            pl.BlockSpec((k, r), lambda i: (0, 0)),
            pl.BlockSpec((r, n), lambda i: (0, 0)),
        ],
        out_specs=pl.BlockSpec((tm, n), lambda i: (i, 0)),
        out_shape=jax.ShapeDtypeStruct((mp + ms, n), F32),
        scratch_shapes=[pltpu.VMEM((k, r), BF16), pltpu.VMEM((r, n), BF16)],
        compiler_params=_params("arbitrary"),
    )(xp, xs, w1, w2)


def _layer_norm_rows(z, g, b):
    mu = jnp.mean(z, axis=-1, keepdims=True)
    zc = z - mu
    var = jnp.mean(zc * zc, axis=-1, keepdims=True)
    return zc * lax.rsqrt(var + LN_EPS) * g + b


def _ln_body(x_ref, mix_ref, g_ref, b_ref, *rest, alpha, with_router):
    if with_router:
        wr_ref, br_ref, o_ref, obf_ref, lg_ref = rest
    else:
        o_ref, obf_ref = rest
    y = _layer_norm_rows(alpha * x_ref[...] + mix_ref[...], g_ref[...], b_ref[...])
    o_ref[...] = y
    obf_ref[...] = y.astype(BF16)
    if with_router:
        lg_ref[...] = jnp.dot(y, wr_ref[...], preferred_element_type=F32,
                              precision=lax.Precision.HIGHEST) + br_ref[...]


def residual_layer_norm(x, mix, g, b, alpha, router=None, tm=256):
    m, d = x.shape
    tm = _div_tile(m, tm)
    row = pl.BlockSpec((tm, d), lambda i: (i, 0))
    vec = pl.BlockSpec((1, d), lambda i: (0, 0))
    in_specs = [row, row, vec, vec]
    args = [x, mix, g.reshape(1, d), b.reshape(1, d)]
    out_specs = [row, row]
    out_shape = [jax.ShapeDtypeStruct((m, d), F32), jax.ShapeDtypeStruct((m, d), BF16)]
    if router is not None:
        wr, br = router
        in_specs += [pl.BlockSpec((d, ROUTE_PAD), lambda i: (0, 0)), pl.BlockSpec((1, ROUTE_PAD), lambda i: (0, 0))]
        args += [wr, br]
        out_specs.append(pl.BlockSpec((tm, ROUTE_PAD), lambda i: (i, 0)))
        out_shape.append(jax.ShapeDtypeStruct((m, ROUTE_PAD), F32))
    return pl.pallas_call(
        functools.partial(_ln_body, alpha=alpha, with_router=router is not None),
        grid=(m // tm,),
        in_specs=in_specs,
        out_specs=out_specs,
        out_shape=out_shape,
        compiler_params=_params("arbitrary"),
    )(*args)


def _lru_gates(xc, gate, wa_ref, wx_ref, ba_ref, bx_ref, lam_ref):
    xcb = xc.astype(BF16)
    r = _sigmoid(jnp.dot(xcb, wa_ref[0].astype(BF16), preferred_element_type=F32) + ba_ref[...])
    i = _sigmoid(jnp.dot(xcb, wx_ref[0].astype(BF16), preferred_element_type=F32) + bx_ref[...])
    log_a = -LRU_C * r * _softplus(-lam_ref[...])
    a = jnp.exp(log_a)
    mult = jnp.sqrt(1.0 - a * a)
    return a, mult * (i * xc), _gelu_tanh(gate)


def _lru_prompt_body(xb_ref, gate_ref, cw_ref, cb_ref, wa_ref, wx_ref, ba_ref, bx_ref, lam_ref,
                     y_ref, tail_ref, h_ref, prev_ref, hc_ref):
    c = pl.program_id(2)

    @pl.when(c == 0)
    def _():
        prev_ref[...] = jnp.zeros_like(prev_ref)
        hc_ref[...] = jnp.zeros_like(hc_ref)

    xb = xb_ref[...]
    tt = xb.shape[0]
    prev8 = prev_ref[...]
    row8 = lax.broadcasted_iota(jnp.int32, prev8.shape, 0)
    cw = cw_ref[...]
    xc = cb_ref[...] + xb * cw[CONV_W - 1:CONV_W]
    for j in range(1, CONV_W):
        rolled = pltpu.roll(xb, j, 0)
        head = jnp.where(row8 < j, pltpu.roll(prev8, j, 0), rolled[:SUBLANES])
        xj = jnp.concatenate([head, rolled[SUBLANES:]], axis=0)
        xc = xc + xj * cw[CONV_W - 1 - j:CONV_W - j]

    a, b, gg = _lru_gates(xc, gate_ref[...], wa_ref, wx_ref, ba_ref, bx_ref, lam_ref)

    row = lax.broadcasted_iota(jnp.int32, a.shape, 0)
    s = 1
    while s < tt:
        keep = row >= s
        b = jnp.where(keep, a * pltpu.roll(b, s, 0) + b, b)
        a = jnp.where(keep, a * pltpu.roll(a, s, 0), a)
        s *= 2
    h = a * hc_ref[...] + b

    y_ref[...] = (h * gg).astype(y_ref.dtype)
    tail_ref[0] = xb[tt - SUBLANES:]
    h_ref[0] = h[tt - 1:]
    prev_ref[...] = xb[tt - SUBLANES:]
    hc_ref[...] = h[tt - 1:]


def lru_prompt(proj, p, j, bsz, t_len, tt=256):
    d = proj.shape[1] // 2
    nb, bw = p['lru_w_a'].shape[1], p['lru_w_a'].shape[2]
    tt = min(tt, t_len)
    assert t_len % tt == 0 and tt % SUBLANES == 0
    nc = t_len // tt
    vec = lambda a: a[j].reshape(1, d)
    vspec = pl.BlockSpec((1, bw), lambda b, n, c: (0, n))
    wspec = pl.BlockSpec((1, bw, bw), lambda b, n, c: (n, 0, 0))
    return pl.pallas_call(
        _lru_prompt_body,
        grid=(bsz, nb, nc),
        in_specs=[
            pl.BlockSpec((tt, bw), lambda b, n, c: (b * nc + c, n)),
            pl.BlockSpec((tt, bw), lambda b, n, c: (b * nc + c, nb + n)),
            pl.BlockSpec((CONV_W, bw), lambda b, n, c: (0, n)),
            vspec, wspec, wspec, vspec, vspec, vspec,
        ],
        out_specs=[
            pl.BlockSpec((tt, bw), lambda b, n, c: (b * nc + c, n)),
            pl.BlockSpec((1, SUBLANES, bw), lambda b, n, c: (b, 0, n)),
            pl.BlockSpec((1, 1, bw), lambda b, n, c: (b, 0, n)),
        ],
        out_shape=[
            jax.ShapeDtypeStruct((bsz * t_len, d), BF16),
            jax.ShapeDtypeStruct((bsz, SUBLANES, d), F32),
            jax.ShapeDtypeStruct((bsz, 1, d), F32),
        ],
        scratch_shapes=[pltpu.VMEM((SUBLANES, bw), F32), pltpu.VMEM((1, bw), F32)],
        compiler_params=_params("arbitrary", "arbitrary", "arbitrary"),
    )(proj, proj, p['lru_conv_w'][j], vec(p['lru_conv_b']), p['lru_w_a'][j], p['lru_w_x'][j],
      vec(p['lru_b_a']), vec(p['lru_b_x']), vec(p['lru_lam']))


def _lru_sample_body(xb_ref, gate_ref, cst_ref, h0_ref, cw_ref, cb_ref, wa_ref, wx_ref, ba_ref, bx_ref,
                     lam_ref, y_ref, tail_ref, h_ref, *, t_len, nseq):
    cw = cw_ref[...]
    xs = [cst_ref[q] for q in range(CONV_W - 1)]
    xs += [xb_ref[t * nseq:(t + 1) * nseq] for t in range(t_len)]
    xcs = []
    for t in range(t_len):
        xc = cb_ref[...] + xs[t] * cw[0:1]
        for q in range(1, CONV_W):
            xc = xc + xs[t + q] * cw[q:q + 1]
        xcs.append(xc)
    a, b, gg = _lru_gates(jnp.concatenate(xcs, axis=0), gate_ref[...], wa_ref, wx_ref, ba_ref, bx_ref, lam_ref)
    h = h0_ref[...]
    for t in range(t_len):
        sl = slice(t * nseq, (t + 1) * nseq)
        h = a[sl] * h + b[sl]
        y_ref[sl, :] = (h * gg[sl]).astype(y_ref.dtype)
    h_ref[...] = h
    for q in range(CONV_W - 1):
        tail_ref[q] = xs[t_len + q]


def lru_sample(proj, conv_st, h0, p, j, mp, nseq, t_len):
    d = proj.shape[1] // 2
    nb, bw = p['lru_w_a'].shape[1], p['lru_w_a'].shape[2]
    ms = nseq * t_len
    assert mp % ms == 0
    blk0 = mp // ms
    vec = lambda a: a[j].reshape(1, d)
    vspec = pl.BlockSpec((1, bw), lambda n: (0, n))
    wspec = pl.BlockSpec((1, bw, bw), lambda n: (n, 0, 0))
    return pl.pallas_call(
        functools.partial(_lru_sample_body, t_len=t_len, nseq=nseq),
        grid=(nb,),
        in_specs=[
            pl.BlockSpec((ms, bw), lambda n: (blk0, n)),
            pl.BlockSpec((ms, bw), lambda n: (blk0, nb + n)),
            pl.BlockSpec((CONV_W - 1, nseq, bw), lambda n: (0, 0, n)),
            pl.BlockSpec((nseq, bw), lambda n: (0, n)),
            pl.BlockSpec((CONV_W, bw), lambda n: (0, n)),
            vspec, wspec, wspec, vspec, vspec, vspec,
        ],
        out_specs=[
            pl.BlockSpec((ms, bw), lambda n: (0, n)),
            pl.BlockSpec((CONV_W - 1, nseq, bw), lambda n: (0, 0, n)),
            pl.BlockSpec((nseq, bw), lambda n: (0, n)),
        ],
        out_shape=[
            jax.ShapeDtypeStruct((ms, d), BF16),
            jax.ShapeDtypeStruct((CONV_W - 1, nseq, d), F32),
            jax.ShapeDtypeStruct((nseq, d), F32),
        ],
        compiler_params=_params("arbitrary"),
    )(proj, proj, conv_st, h0, p['lru_conv_w'][j], vec(p['lru_conv_b']), p['lru_w_a'][j], p['lru_w_x'][j],
      vec(p['lru_b_a']), vec(p['lru_b_x']), vec(p['lru_lam']))


def _mix_store(x, prev, mu_ref, out_refs):
    xx = prev - x
    for q, o_ref in enumerate(out_refs):
        o_ref[...] = (x + xx * mu_ref[q:q + 1]).astype(o_ref.dtype)


def _shift_prompt_body(x_ref, mu_ref, *rest, blocks_per_seq):
    out_refs, carry_ref = rest[:-1], rest[-1]
    i = pl.program_id(1)
    x = x_ref[...]
    first = jnp.where(i % blocks_per_seq == 0, jnp.zeros_like(carry_ref[...]), carry_ref[...])
    row = lax.broadcasted_iota(jnp.int32, x.shape, 0)
    prev = jnp.where(row == 0, first, pltpu.roll(x, 1, 0))
    _mix_store(x, prev, mu_ref, out_refs)
    carry_ref[...] = x[x.shape[0] - 1:]


def shift_prompt(x, mu, mp, t_len, tm=256, tc=1024):
    d = x.shape[1]
    tm, tc = min(tm, t_len), min(tc, d)
    assert t_len % tm == 0 and d % tc == 0
    nq = mu.shape[0]
    spec = pl.BlockSpec((tm, tc), lambda c, i: (i, c))
    return pl.pallas_call(
        functools.partial(_shift_prompt_body, blocks_per_seq=t_len // tm),
        grid=(d // tc, mp // tm),
        in_specs=[spec, pl.BlockSpec((nq, tc), lambda c, i: (0, c))],
        out_specs=[spec] * nq,
        out_shape=[jax.ShapeDtypeStruct((mp, d), BF16)] * nq,
        scratch_shapes=[pltpu.VMEM((1, tc), F32)],
        compiler_params=_params("arbitrary", "arbitrary"),
    )(x, mu)


def _shift_sample_body(x_ref, st_ref, mu_ref, *out_refs, nseq):
    x = x_ref[...]
    prev = jnp.concatenate([st_ref[...], x[:x.shape[0] - nseq]], axis=0)
    _mix_store(x, prev, mu_ref, out_refs)


def shift_sample(x, state, mu, mp, nseq, t_len, tc=1024):
    d = x.shape[1]
    tc = min(tc, d)
    ms = nseq * t_len
    blk0 = mp // ms
    nq = mu.shape[0]
    return pl.pallas_call(
        functools.partial(_shift_sample_body, nseq=nseq),
        grid=(d // tc,),
        in_specs=[pl.BlockSpec((ms, tc), lambda c: (blk0, c)),
                  pl.BlockSpec((nseq, tc), lambda c: (0, c)),
                  pl.BlockSpec((nq, tc), lambda c: (0, c))],
        out_specs=[pl.BlockSpec((ms, tc), lambda c: (0, c))] * nq,
        out_shape=[jax.ShapeDtypeStruct((ms, d), BF16)] * nq,
        compiler_params=_params("arbitrary"),
    )(x, state, mu)


def _seg_sum(x):
    lo = lax.broadcasted_iota(jnp.int32, x.shape, x.ndim - 1) < RWKV_HEAD
    s_lo = jnp.sum(jnp.where(lo, x, 0.0), axis=-1, keepdims=True)
    s_hi = jnp.sum(jnp.where(lo, 0.0, x), axis=-1, keepdims=True)
    return jnp.where(lo, s_lo, s_hi)


def _wkv_parse(refs, has_vres, has_state, n_scratch):
    it = iter(refs)
    acts = tuple(next(it) for _ in range(6))
    vres = tuple(next(it) for _ in range(3)) if has_vres else None
    vecs = tuple(next(it) for _ in range(7))
    s0_ref = next(it) if has_state else None
    y_ref, s_ref = next(it), next(it)
    scratch = tuple(next(it) for _ in range(n_scratch))
    return acts, vres, vecs, s0_ref, y_ref, s_ref, scratch


def _wkv_operands(acts, vres, vecs, n_pairs):
    r_ref, k_ref, v_ref, wr_ref, ar_ref, _ = acts
    w0_ref, a0_ref, kk_ref, ka_ref, rk_ref, _, _ = vecs
    r = r_ref[...]
    k = k_ref[...]
    v = v_ref[...]
    w = -_softplus(-(w0_ref[...] + wr_ref[...])) - 0.5
    a = _sigmoid(a0_ref[...] + ar_ref[...])
    if vres is not None:
        vr_ref, vf_ref, v0_ref = vres
        v = v + (vf_ref[...] - v) * _sigmoid(v0_ref[...] + vr_ref[...])
    k2 = k * (1.0 + (a - 1.0) * ka_ref[...])
    kks = []
    for pr in range(n_pairs):
        ls = slice(pr * PAIR, (pr + 1) * PAIR)
        kkr = k[:, ls] * kk_ref[:, ls]
        kks.append(kkr / jnp.maximum(jnp.sqrt(_seg_sum(kkr * kkr)), 1e-12))
    kk = kks[0] if n_pairs == 1 else jnp.concatenate(kks, axis=1)
    return r, v, k2, kk, kk * a, -jnp.exp(w), r * k2 * rk_ref[...]


def _wkv_finish(ys_s, bonus_terms, v, acts, vecs, y_ref, n_pairs):
    g_ref, lg_ref, lb_ref = acts[5], vecs[5], vecs[6]
    for pr in range(n_pairs):
        ls = slice(pr * PAIR, (pr + 1) * PAIR)
        y = ys_s[:, ls]
        mu = _seg_sum(y) * (1.0 / RWKV_HEAD)
        yc = y - mu
        var = _seg_sum(yc * yc) * (1.0 / RWKV_HEAD)
        yn = yc * lax.rsqrt(var + GN_EPS) * lg_ref[:, ls] + lb_ref[:, ls]
        bonus = _seg_sum(bonus_terms[:, ls]) * v[:, ls]
        y_ref[:, ls] = ((yn + bonus) * g_ref[:, ls]).astype(y_ref.dtype)


def _wkv_steps_body(*refs, nb, nt, n_pairs, has_vres):
    acts, vres, vecs, s0_ref, y_ref, s_ref, scratch = _wkv_parse(refs, has_vres, True, 6)
    dec_s, kkn_s, kab_s, k2_s, v_s, ys_s = scratch
    r_ref = acts[0]
    s_ref[...] = s0_ref[...]
    _, v, k2, kk, kka, logdec, bonus_terms = _wkv_operands(acts, vres, vecs, n_pairs)
    dec_s[...] = jnp.exp(logdec)
    kkn_s[...] = -kk
    kab_s[...] = kka
    k2_s[...] = k2
    v_s[...] = v

    eye = (lax.broadcasted_iota(jnp.int32, (RWKV_HEAD, PAIR), 1) % RWKV_HEAD
           == lax.broadcasted_iota(jnp.int32, (RWKV_HEAD, PAIR), 0))
    sub8 = lax.broadcasted_iota(jnp.int32, (SUBLANES, PAIR), 0)
    operand_refs = (kkn_s, v_s, dec_s, kab_s, k2_s, r_ref)

    def step(s, tiles, i, ytile):
        kkn, vv, dec, kab, k2r, rr = (t[i:i + 1] for t in tiles)
        sa = _seg_sum(s * kkn)
        vcol = _seg_sum(jnp.where(eye, vv, 0.0))
        s = s * dec + sa * kab + vcol * k2r
        ycol = _seg_sum(s * rr)
        yrow = jnp.sum(jnp.where(eye, ycol, 0.0), axis=0, keepdims=True)
        return s, jnp.where(sub8 == i, yrow, ytile)

    for t in range(nt):
        def seq_group(gi, carry):
            b0 = gi * SUBLANES
            rows = pl.ds(pl.multiple_of(t * nb + b0, SUBLANES), SUBLANES)
            for pr in range(n_pairs):
                ls = slice(pr * PAIR, (pr + 1) * PAIR)
                tiles = tuple(ref[rows, ls] for ref in operand_refs)
                states = [s_ref[b0 + u, pr] for u in range(SUBLANES)]
                ytile = jnp.zeros((SUBLANES, PAIR), F32)
                for u in range(SUBLANES):
                    states[u], ytile = step(states[u], tiles, u, ytile)
                for u in range(SUBLANES):
                    s_ref[b0 + u, pr] = states[u]
                ys_s[rows, ls] = ytile
            return carry

        lax.fori_loop(0, nb // SUBLANES, seq_group, 0)

    _wkv_finish(ys_s, bonus_terms, v, acts, vecs, y_ref, n_pairs)


def _bdot(a, b, dims=(((1,), (0,)), ((), ()))):
    return lax.dot_general(a.astype(BF16), b.astype(BF16), dims, preferred_element_type=F32)


_NT = (((1,), (1,)), ((), ()))
_TN = (((0,), (0,)), ((), ()))


def _wkv_chunk_body(*refs, n_pairs, has_vres, chunk):
    acts, vres, vecs, _, y_ref, z_ref, scratch = _wkv_parse(refs, has_vres, False, 6)
    ld_s, kk_s, ka_s, k2_s, v_s, ys_s = scratch
    r_ref = acts[0]

    @pl.when(pl.program_id(2) == 0)
    def _():
        z_ref[...] = jnp.zeros_like(z_ref)

    _, v, k2, kk, kka, logdec, bonus_terms = _wkv_operands(acts, vres, vecs, n_pairs)
    ld_s[...] = logdec
    kk_s[...] = kk
    ka_s[...] = kka
    k2_s[...] = k2
    v_s[...] = v

    row_c = lax.broadcasted_iota(jnp.int32, (chunk, PAIR), 0)
    lo_c = lax.broadcasted_iota(jnp.int32, (chunk, PAIR), 1) < RWKV_HEAD
    i2 = lax.broadcasted_iota(jnp.int32, (2 * chunk, 2 * chunk), 0)
    j2 = lax.broadcasted_iota(jnp.int32, (2 * chunk, 2 * chunk), 1)
    same = (i2 // chunk) == (j2 // chunk)
    strict = same & ((i2 % chunk) > (j2 % chunk))
    incl = same & ((i2 % chunk) >= (j2 % chunk))
    ik = lax.broadcasted_iota(jnp.int32, (PAIR, PAIR), 0)
    jk = lax.broadcasted_iota(jnp.int32, (PAIR, PAIR), 1)
    n2 = 2 * chunk

    def stack2(x):
        return jnp.concatenate([jnp.where(lo_c, x, 0.0), jnp.where(lo_c, 0.0, x)], axis=0)

    def chunk_step(ci, carry):
        rows = pl.ds(pl.multiple_of(ci * chunk, chunk), chunk)
        pairs = range(n_pairs)
        lss = [slice(pr * PAIR, (pr + 1) * PAIR) for pr in pairs]
        qk, qr, kcat, kag, kkg, vb, g_col = ([] for _ in range(7))
        for ls in lss:
            ld = ld_s[rows, ls]
            cum = ld
            s = 1
            while s < chunk:
                cum = cum + jnp.where(row_c >= s, pltpu.roll(cum, s, 0), 0.0)
                s *= 2
            lwc = cum[chunk - 1:chunk]
            kk_c, ka_c, k2_c = kk_s[rows, ls], ka_s[rows, ls], k2_s[rows, ls]
            e_inv = jnp.exp(-cum)
            e_end = jnp.exp(lwc - cum)
            qk.append(stack2(kk_c * jnp.exp(cum - ld)).astype(BF16))
            qr.append(stack2(r_ref[rows, ls] * jnp.exp(cum)))
            kcat.append(jnp.concatenate([stack2(ka_c * e_inv), stack2(k2_c * e_inv)], axis=0).astype(BF16))
            kag.append(stack2(ka_c * e_end).astype(BF16))
            kkg.append(stack2(k2_c * e_end).astype(BF16))
            vb.append(stack2(v_s[rows, ls]).astype(BF16))
            g_col.append(jnp.sum(jnp.where(ik == jk, jnp.exp(lwc), 0.0), axis=1, keepdims=True))
        sc = [_bdot(jnp.concatenate([qk[p], qr[p].astype(BF16)], axis=0), kcat[p], _NT) for p in pairs]
        a_ak = [jnp.where(strict, sc[p][:n2, :n2], 0.0) for p in pairs]
        a_kk = [jnp.where(strict, sc[p][:n2, n2:], 0.0).astype(BF16) for p in pairs]
        b_ak = [jnp.where(incl, sc[p][n2:, :n2], 0.0).astype(BF16) for p in pairs]
        b_kk = [jnp.where(incl, sc[p][n2:, n2:], 0.0).astype(BF16) for p in pairs]
        tinv = [jnp.where(i2 == j2, 1.0, 0.0) - a for a in a_ak]
        apow = [a.astype(BF16) for a in a_ak]
        s = 1
        while 2 * s < chunk:
            apow = [_bdot(a, a).astype(BF16) for a in apow]
            tinv = [t + _bdot(t, a) for t, a in zip(tinv, apow)]
            s *= 2
        w = [_bdot(a_kk[p], vb[p]).astype(BF16) for p in pairs]
        pu = [(-_bdot(tinv[p], jnp.concatenate([qk[p], w[p]], axis=1))).astype(BF16) for p in pairs]
        bp = [_bdot(b_ak[p], pu[p]) for p in pairs]
        kp = [_bdot(kag[p], pu[p], _TN) for p in pairs]
        y0 = [bp[p][:, PAIR:] + _bdot(b_kk[p], vb[p]) for p in pairs]
        z0 = [kp[p][:, PAIR:] + _bdot(kkg[p], vb[p], _TN) for p in pairs]
        for p in pairs:
            z = z_ref[0, p]
            zb = z.astype(BF16)
            yb = _bdot(qr[p] + bp[p][:, :PAIR], zb) + y0[p]
            ys_s[rows, lss[p]] = yb[:chunk] + yb[chunk:]
            z_ref[0, p] = g_col[p] * z + _bdot(kp[p][:, :PAIR], zb) + z0[p]
        return carry

    lax.fori_loop(0, r_ref.shape[0] // chunk, chunk_step, 0)
    _wkv_finish(ys_s, bonus_terms, v, acts, vecs, y_ref, n_pairs)


def _wkv_call(body, acts, vres, p, j, state, *, grid, rows, n_pairs, act_map, out_map, vec_map, st_map,
              out_rows, state_shape, state_block):
    d = acts[0].shape[1]
    lw = n_pairs * PAIR
    aspec = pl.BlockSpec((rows, lw), act_map)
    vspec = pl.BlockSpec((1, lw), vec_map)
    sspec = pl.BlockSpec(state_block, st_map)
    vec = lambda a: a.reshape(1, d)
    args, in_specs = list(acts), [aspec] * 6
    if vres is not None:
        v_raw, v_first, v0 = vres
        args += [v_raw, v_first, vec(v0)]
        in_specs += [aspec, aspec, vspec]
    args += [vec(p['rw_w0'][j]), vec(p['rw_a0'][j]), vec(p['rw_k_k'][j]), vec(p['rw_k_a'][j]),
             vec(p['rw_r_k'][j]), vec(p['rw_lnx_g'][j]), vec(p['rw_lnx_b'][j])]
    in_specs += [vspec] * 7
    if state is not None:
        args.append(state)
        in_specs.append(sspec)
    return pl.pallas_call(
        body,
        grid=grid,
        in_specs=in_specs,
        out_specs=[pl.BlockSpec((rows, lw), out_map), sspec],
        out_shape=[jax.ShapeDtypeStruct((out_rows, d), BF16), jax.ShapeDtypeStruct(state_shape, F32)],
        scratch_shapes=[pltpu.VMEM((rows, lw), F32)] * 6,
        compiler_params=_params(*(["arbitrary"] * len(grid))),
    )(*args)


def wkv_prompt(acts, vres, p, j, bsz, t_len, n_pairs=8, rows=256, chunk=64):
    d = acts[0].shape[1]
    n_pairs = min(n_pairs, d // PAIR)
    rows = min(rows, t_len)
    assert t_len % rows == 0 and rows % chunk == 0
    nc = t_len // rows
    return _wkv_call(
        functools.partial(_wkv_chunk_body, n_pairs=n_pairs, has_vres=vres is not None, chunk=chunk),
        acts, vres, p, j, None, grid=(bsz, d // (n_pairs * PAIR), nc), rows=rows, n_pairs=n_pairs,
        act_map=lambda b, l, c: (b * nc + c, l), out_map=lambda b, l, c: (b * nc + c, l),
        vec_map=lambda b, l, c: (0, l), st_map=lambda b, l, c: (b, l, 0, 0),
        out_rows=bsz * t_len, state_shape=(bsz, d // PAIR, PAIR, PAIR), state_block=(1, n_pairs, PAIR, PAIR))


def wkv_sample(acts, vres, p, j, state, row_blk0, nseq, t_len):
    d = acts[0].shape[1]
    return _wkv_call(
        functools.partial(_wkv_steps_body, nb=nseq, nt=t_len, n_pairs=1, has_vres=vres is not None),
        acts, vres, p, j, state, grid=(d // PAIR,), rows=nseq * t_len, n_pairs=1,
        act_map=lambda l: (row_blk0, l), out_map=lambda l: (0, l), vec_map=lambda l: (0, l),
        st_map=lambda l: (0, l, 0, 0), out_rows=nseq * t_len, state_shape=(nseq, d // PAIR, RWKV_HEAD, PAIR),
        state_block=(nseq, 1, RWKV_HEAD, PAIR))


def _pair_state(s):
    b, h, nv, nk = s.shape
    return s.reshape(b, h // 2, 2, nv, nk).transpose(0, 1, 3, 2, 4).reshape(b, h // 2, nv, 2 * nk)


def _unpair_state(s):
    b, hp, nv, nk2 = s.shape
    return s.reshape(b, hp, nv, 2, nk2 // 2).transpose(0, 1, 3, 2, 4).reshape(b, hp * 2, nv, nk2 // 2)


def _state_from_blockdiag(z):
    b, hp = z.shape[:2]
    z = z.reshape(b, hp, 2, RWKV_HEAD, 2, RWKV_HEAD)
    diag = jnp.stack([z[:, :, 0, :, 0, :], z[:, :, 1, :, 1, :]], axis=2)
    return diag.transpose(0, 1, 2, 4, 3).reshape(b, hp * 2, RWKV_HEAD, RWKV_HEAD)


def _route(logits, n_groups, n_experts, tm):
    m = logits.shape[0]
    epg = n_experts // n_groups
    g_logits = logits[:, :n_groups]
    g_idx = jnp.argmax(g_logits, axis=-1)
    p_group = jnp.max(jax.nn.softmax(g_logits, axis=-1), axis=-1, keepdims=True)
    e_logits = logits[:, n_groups:n_groups + n_experts].reshape(m, n_groups, epg)
    e_in = jnp.take_along_axis(e_logits, g_idx[:, None, None], axis=1)[:, 0]
    top_val, top_idx = lax.top_k(e_in, TOP_K)
    gate = p_group * jax.nn.softmax(top_val, axis=-1)
    expert_flat = (g_idx[:, None] * epg + top_idx).astype(jnp.int32).reshape(-1)

    n_assign = m * TOP_K
    order = jnp.argsort(expert_flat)
    e_sorted = expert_flat[order]
    tok_sorted = (order // TOP_K).astype(jnp.int32)
    counts = jnp.bincount(expert_flat, length=n_experts).astype(jnp.int32)
    start = jnp.cumsum(counts) - counts
    nblk_e = (counts + tm - 1) // tm
    blk_end = jnp.cumsum(nblk_e)
    blk_start = blk_end - nblk_e
    slot_sorted = blk_start[e_sorted] * tm + (jnp.arange(n_assign, dtype=jnp.int32) - start[e_sorted])
    n_blocks = (n_assign + n_experts * (tm - 1) + tm - 1) // tm
    slot_tok = jnp.zeros((n_blocks * tm,), jnp.int32).at[slot_sorted].set(tok_sorted)
    slot_of_assign = jnp.zeros((n_assign,), jnp.int32).at[order].set(slot_sorted)
    nblk_e = nblk_e.at[n_experts - 1].add(n_blocks - blk_end[-1])
    return slot_tok, slot_of_assign.reshape(m, TOP_K), gate.astype(F32), blk_start, nblk_e, n_blocks


def _items(blk_start, nblk_e, n_blocks, n_chunks):
    n_items = n_blocks * n_chunks
    it = jnp.arange(n_items, dtype=jnp.int32)
    item_end = jnp.cumsum(nblk_e * n_chunks)
    e = jnp.minimum(jnp.searchsorted(item_end, it, side='right'), nblk_e.shape[0] - 1).astype(jnp.int32)
    local = it - (item_end[e] - nblk_e[e] * n_chunks)
    nb_e = jnp.maximum(nblk_e[e], 1)
    return e, (local // nb_e).astype(jnp.int32), (blk_start[e] + local % nb_e).astype(jnp.int32)


def _gather_body(tok_ref, x_hbm, o_ref, buf_ref, sem, *, tm):
    base = pl.program_id(0) * tm

    def copy(r):
        return pltpu.make_async_copy(x_hbm.at[pl.ds(tok_ref[base + r], 1)], buf_ref.at[pl.ds(r, 1)], sem)

    def start(r, c):
        copy(r).start()
        return c

    def wait(r, c):
        copy(r).wait()
        return c

    lax.fori_loop(0, tm, start, 0)
    lax.fori_loop(0, tm, wait, 0)
    o_ref[...] = buf_ref[...].astype(o_ref.dtype)


def gather_rows(x, slot_tok, tm):
    d = x.shape[1]
    n_slots = slot_tok.shape[0]
    return pl.pallas_call(
        functools.partial(_gather_body, tm=tm),
        grid_spec=pltpu.PrefetchScalarGridSpec(
            num_scalar_prefetch=1,
            grid=(n_slots // tm,),
            in_specs=[pl.BlockSpec(memory_space=pl.ANY)],
            out_specs=pl.BlockSpec((tm, d), lambda i, tok: (i, 0)),
            scratch_shapes=[pltpu.VMEM((tm, d), F32), pltpu.SemaphoreType.DMA(())],
        ),
        out_shape=jax.ShapeDtypeStruct((n_slots, d), BF16),
        compiler_params=_params("arbitrary"),
    )(slot_tok, x)


def _ffn_up_body(e_ref, j_ref, b_ref, x_ref, wg_ref, wu_ref, o_ref, wgbf_ref, wubf_ref):
    it = pl.program_id(0)
    prev = jnp.maximum(it - 1, 0)
    fresh = (it == 0) | (e_ref[it] != e_ref[prev]) | (j_ref[it] != j_ref[prev])

    @pl.when(fresh)
    def _():
        wgbf_ref[...] = wg_ref[...].astype(BF16)
        wubf_ref[...] = wu_ref[...].astype(BF16)

    x = x_ref[...]
    hg = jnp.dot(x, wgbf_ref[...], preferred_element_type=F32)
    hu = jnp.dot(x, wubf_ref[...], preferred_element_type=F32)
    o_ref[...] = (hg * _sigmoid(hg) * hu).astype(o_ref.dtype)


def _ffn_down_body(e_ref, j_ref, b_ref, h_ref, wd_ref, o_ref, wdbf_ref):
    it = pl.program_id(0)
    prev = jnp.maximum(it - 1, 0)
    fresh = (it == 0) | (e_ref[it] != e_ref[prev]) | (j_ref[it] != j_ref[prev])

    @pl.when(fresh)
    def _():
        wdbf_ref[...] = wd_ref[...].astype(BF16)

    o_ref[...] = jnp.dot(h_ref[...], wdbf_ref[...], preferred_element_type=F32)


def expert_ffn(x_sorted, blk_start, nblk_e, n_blocks, w_gate, w_up, w_down, layer, tm, tn_up=512, tn_down=2048):
    n_slots, d = x_sorted.shape
    de = w_gate.shape[3]
    tn_up, tn_down = min(tn_up, de), min(tn_down, d)
    e1, j1, b1 = _items(blk_start, nblk_e, n_blocks, de // tn_up)
    hid = pl.pallas_call(
        _ffn_up_body,
        grid_spec=pltpu.PrefetchScalarGridSpec(
            num_scalar_prefetch=3,
            grid=(e1.shape[0],),
            in_specs=[pl.BlockSpec((tm, d), lambda it, e, j, b: (b[it], 0)),
                      pl.BlockSpec((None, None, d, tn_up), lambda it, e, j, b: (layer, e[it], 0, j[it])),
                      pl.BlockSpec((None, None, d, tn_up), lambda it, e, j, b: (layer, e[it], 0, j[it]))],
            out_specs=pl.BlockSpec((tm, tn_up), lambda it, e, j, b: (b[it], j[it])),
            scratch_shapes=[pltpu.VMEM((d, tn_up), BF16), pltpu.VMEM((d, tn_up), BF16)],
        ),
        out_shape=jax.ShapeDtypeStruct((n_slots, de), BF16),
        compiler_params=_params("arbitrary"),
    )(e1, j1, b1, x_sorted, w_gate, w_up)
    e2, j2, b2 = _items(blk_start, nblk_e, n_blocks, d // tn_down)
    return pl.pallas_call(
        _ffn_down_body,
        grid_spec=pltpu.PrefetchScalarGridSpec(
            num_scalar_prefetch=3,
            grid=(e2.shape[0],),
            in_specs=[pl.BlockSpec((tm, de), lambda it, e, j, b: (b[it], 0)),
                      pl.BlockSpec((None, None, de, tn_down), lambda it, e, j, b: (layer, e[it], 0, j[it]))],
            out_specs=pl.BlockSpec((tm, tn_down), lambda it, e, j, b: (b[it], j[it])),
            scratch_shapes=[pltpu.VMEM((de, tn_down), BF16)],
        ),
        out_shape=jax.ShapeDtypeStruct((n_slots, d), F32),
        compiler_params=_params("arbitrary"),
    )(e2, j2, b2, hid, w_down)


def _combine_ln_body(slot_ref, x_ref, gate_ref, g_ref, b_ref, y_hbm, o_ref, obf_ref, buf_ref, sem, *, tm, alpha):
    base = pl.program_id(0) * tm

    def copy(q):
        return pltpu.make_async_copy(y_hbm.at[pl.ds(slot_ref[base * TOP_K + q], 1)],
                                     buf_ref.at[q % TOP_K, pl.ds(q // TOP_K, 1)], sem)

    def start(q, c):
        copy(q).start()
        return c

    def wait(q, c):
        copy(q).wait()
        return c

    lax.fori_loop(0, tm * TOP_K, start, 0)
    lax.fori_loop(0, tm * TOP_K, wait, 0)
    gate = gate_ref[...]
    ffn = buf_ref[0] * gate[:, 0:1]
    for q in range(1, TOP_K):
        ffn = ffn + buf_ref[q] * gate[:, q:q + 1]
    y = _layer_norm_rows(alpha * x_ref[...] + ffn, g_ref[...], b_ref[...])
    o_ref[...] = y
    obf_ref[...] = y.astype(BF16)


def combine_layer_norm(x, out_slots, slot_of_assign, gate, g, b, alpha, tm=256):
    m, d = x.shape
    tm = _div_tile(m, tm)
    row = pl.BlockSpec((tm, d), lambda i, s: (i, 0))
    vec = pl.BlockSpec((1, d), lambda i, s: (0, 0))
    return pl.pallas_call(
        functools.partial(_combine_ln_body, tm=tm, alpha=alpha),
        grid_spec=pltpu.PrefetchScalarGridSpec(
            num_scalar_prefetch=1,
            grid=(m // tm,),
            in_specs=[row, pl.BlockSpec((tm, TOP_K), lambda i, s: (i, 0)), vec, vec,
                      pl.BlockSpec(memory_space=pl.ANY)],
            out_specs=[row, row],
            scratch_shapes=[pltpu.VMEM((TOP_K, tm, d), F32), pltpu.SemaphoreType.DMA(())],
        ),
        out_shape=[jax.ShapeDtypeStruct((m, d), F32), jax.ShapeDtypeStruct((m, d), BF16)],
        compiler_params=_params("arbitrary"),
    )(slot_of_assign.reshape(-1), x, gate, g.reshape(1, d), b.reshape(1, d), out_slots)


def _trunk(x_prompt, x_sample, state_conv, state_h, state_shift, state_wkv, p, moe_tm=128):
    bsz, t_len, d = x_prompt.shape
    nseq, dec_t, _ = x_sample.shape
    mp, ms = bsz * t_len, nseq * dec_t
    depth = p['ln_g'].shape[0]
    alpha = (2 * depth) ** 0.25
    n_groups = p['moe_w_group'].shape[2]
    n_experts = p['moe_w_expert'].shape[2]

    x = jnp.concatenate([x_prompt.reshape(mp, d), x_sample.transpose(1, 0, 2).reshape(ms, d)], axis=0)
    xbf = x.astype(BF16)
    conv_s_tm = state_conv.transpose(0, 2, 1, 3)

    conv_p, h_p, shift_p, wkv_p, conv_s, h_s, shift_s, wkv_s = ([] for _ in range(8))
    v_first = None
    for i in range(depth):
        j = i // 2
        if i % 2 == 0:
            proj = matmul(xbf, p['lru_w_in'], j, mp, ms)
            y_p, tail_p, hl_p = lru_prompt(proj, p, j, bsz, t_len)
            y_s, tail_s, hl_s = lru_sample(proj, conv_s_tm[j], state_h[j], p, j, mp, nseq, dec_t)
            mix = matmul((y_p, y_s), p['lru_w_out'], j, mp, ms)
            conv_p.append(tail_p[:, SUBLANES - (CONV_W - 1):])
            h_p.append(hl_p[:, 0])
            conv_s.append(tail_s.transpose(1, 0, 2))
            h_s.append(hl_s)
        else:
            mu = p['rw_mu'][j]
            mixed_p = shift_prompt(x, mu, mp, t_len)
            mixed_s = shift_sample(x, state_shift[j], mu, mp, nseq, dec_t)
            xr, xw, xk, xv, xa, xg = zip(mixed_p, mixed_s)
            r = matmul(xr, p['rw_w_r'], j, mp, ms)
            k = matmul(xk, p['rw_w_k'], j, mp, ms)
            v = matmul(xv, p['rw_w_v'], j, mp, ms)
            w_raw = lora(xw, p['rw_w1'][j], p['rw_w2'][j], mp, ms, "tanh")
            a_raw = lora(xa, p['rw_a1'][j], p['rw_a2'][j], mp, ms, "none")
            g = lora(xg, p['rw_g1'][j], p['rw_g2'][j], mp, ms, "sigmoid")
            if j == 0:
                vres, v_first = None, v
            else:
                v_raw = lora(xv, p['rw_v1'][j - 1], p['rw_v2'][j - 1], mp, ms, "none")
                vres = (v_raw, v_first, p['rw_v0'][j - 1])
            acts = (r, k, v, w_raw, a_raw, g)
            yg_p, z_p = wkv_prompt(acts, vres, p, j, bsz, t_len)
            yg_s, st_s = wkv_sample(acts, vres, p, j, _pair_state(state_wkv[j]), mp // ms, nseq, dec_t)
            mix = matmul((yg_p, yg_s), p['rw_w_o'], j, mp, ms)
            shift_p.append(x[t_len - 1:mp:t_len])
            shift_s.append(x[mp + (dec_t - 1) * nseq:])
            wkv_p.append(_state_from_blockdiag(z_p))
            wkv_s.append(_unpair_state(st_s))

        w_route = jnp.concatenate([p['moe_w_group'][i], p['moe_w_expert'][i]], axis=1)
        b_route = jnp.concatenate([p['moe_b_group'][i], p['moe_b_expert'][i]], axis=0)
        pad = ROUTE_PAD - w_route.shape[1]
        w_route = jnp.pad(w_route, ((0, 0), (0, pad)))
        b_route = jnp.pad(b_route, (0, pad)).reshape(1, ROUTE_PAD)
        x, xbf, logits = residual_layer_norm(x, mix, p['ln_g'][i, 0], p['ln_b'][i, 0], alpha,
                                             router=(w_route, b_route))
        slot_tok, slot_of_assign, gate, blk_start, nblk_e, n_blocks = _route(logits, n_groups, n_experts, moe_tm)
        x_sorted = gather_rows(x, slot_tok, moe_tm)
        out_slots = expert_ffn(x_sorted, blk_start, nblk_e, n_blocks, p['moe_w_gate'], p['moe_w_up'],
                               p['moe_w_down'], i, moe_tm)
        x, xbf = combine_layer_norm(x, out_slots, slot_of_assign, gate, p['ln_g'][i, 1], p['ln_b'][i, 1], alpha)

    y_prompt = x[:mp].reshape(bsz, t_len, d)
    y_sample = x[mp:].reshape(dec_t, nseq, d).transpose(1, 0, 2)
    return (y_prompt, y_sample, jnp.stack(conv_p), jnp.stack(h_p), jnp.stack(shift_p), jnp.stack(wkv_p),
            jnp.stack(conv_s), jnp.stack(h_s), jnp.stack(shift_s), jnp.stack(wkv_s))


def kernel(x_prompt, x_sample, state_conv, state_h, state_shift, state_wkv, ln_g, ln_b, lru_w_in, lru_conv_w, lru_conv_b, lru_w_a, lru_b_a, lru_w_x, lru_b_x, lru_lam, lru_w_out, rw_mu, rw_w_r, rw_w_k, rw_w_v, rw_w_o, rw_w0, rw_w1, rw_w2, rw_a0, rw_a1, rw_a2, rw_v0, rw_v1, rw_v2, rw_g1, rw_g2, rw_k_k, rw_k_a, rw_r_k, rw_lnx_g, rw_lnx_b, moe_w_group, moe_b_group, moe_w_expert, moe_b_expert, moe_w_gate, moe_w_up, moe_w_down):
    p = dict(ln_g=ln_g, ln_b=ln_b, lru_w_in=lru_w_in, lru_conv_w=lru_conv_w, lru_conv_b=lru_conv_b,
             lru_w_a=lru_w_a, lru_b_a=lru_b_a, lru_w_x=lru_w_x, lru_b_x=lru_b_x, lru_lam=lru_lam,
             lru_w_out=lru_w_out, rw_mu=rw_mu, rw_w_r=rw_w_r, rw_w_k=rw_w_k, rw_w_v=rw_w_v, rw_w_o=rw_w_o,
             rw_w0=rw_w0, rw_w1=rw_w1, rw_w2=rw_w2, rw_a0=rw_a0, rw_a1=rw_a1, rw_a2=rw_a2,
             rw_v0=rw_v0, rw_v1=rw_v1, rw_v2=rw_v2, rw_g1=rw_g1, rw_g2=rw_g2, rw_k_k=rw_k_k, rw_k_a=rw_k_a,
             rw_r_k=rw_r_k, rw_lnx_g=rw_lnx_g, rw_lnx_b=rw_lnx_b, moe_w_group=moe_w_group,
             moe_b_group=moe_b_group, moe_w_expert=moe_w_expert, moe_b_expert=moe_b_expert,
             moe_w_gate=moe_w_gate, moe_w_up=moe_w_up, moe_w_down=moe_w_down)
    return _trunk(x_prompt, x_sample, state_conv, state_h, state_shift, state_wkv, p)
```

```python
import functools

import jax
import jax.numpy as jnp
from jax import lax
from jax.experimental import pallas as pl
from jax.experimental.pallas import tpu as pltpu

F32 = jnp.float32
BF16 = jnp.bfloat16

V7X_VMEM_LIMIT_BYTES = 56 * 1024 * 1024
LANES = 128
SUBLANES = 8
RWKV_HEAD = 64
PAIR = 2 * RWKV_HEAD
CONV_W = 4
LRU_C = 8.0
GN_EPS = 64e-5
LN_EPS = 1e-5
TOP_K = 2
ROUTE_PAD = LANES


def _params(*sem):
    return pltpu.CompilerParams(dimension_semantics=sem, vmem_limit_bytes=V7X_VMEM_LIMIT_BYTES)


def _sigmoid(x):
    return 1.0 / (1.0 + jnp.exp(-x))


def _softplus(x):
    return jnp.maximum(x, 0.0) + jnp.log1p(jnp.exp(-jnp.abs(x)))


def _gelu_tanh(x):
    return 0.5 * x * (1.0 + jnp.tanh(0.7978845608028654 * (x + 0.044715 * (x * x * x))))


def _row_tile(mp, ms, cap):
    tm = min(cap, ms)
    assert mp % tm == 0 and ms % tm == 0 and tm % SUBLANES == 0, (mp, ms, tm)
    return tm


def _div_tile(m, cap):
    tm = min(cap, m) // SUBLANES * SUBLANES
    while m % tm:
        tm -= SUBLANES
    return tm


def _mm_body(xp_ref, xs_ref, w_ref, o_ref, wbf_ref, *, n_p):
    i = pl.program_id(1)

    @pl.when(i == 0)
    def _():
        wbf_ref[...] = w_ref[...].astype(BF16)

    def run(x_ref):
        o_ref[...] = jnp.dot(x_ref[...], wbf_ref[...], preferred_element_type=F32).astype(o_ref.dtype)

    @pl.when(i < n_p)
    def _():
        run(xp_ref)

    @pl.when(i >= n_p)
    def _():
        run(xs_ref)


def _split_rows(act, mp):
    if isinstance(act, tuple):
        return act[0], act[1], 0
    return act, act, mp


def matmul(act, w, layer, mp, ms, out_dtype=F32, tn=512, tm_cap=512):
    xp, xs, s_off = _split_rows(act, mp)
    _, k, n = w.shape
    tn = min(tn, n)
    assert n % tn == 0
    tm = _row_tile(mp, ms, tm_cap)
    n_p, n_s = mp // tm, ms // tm
    s_blk0 = s_off // tm
    return pl.pallas_call(
        functools.partial(_mm_body, n_p=n_p),
        grid=(n // tn, n_p + n_s),
        in_specs=[
            pl.BlockSpec((tm, k), lambda j, i: (jnp.minimum(i, n_p - 1), 0)),
            pl.BlockSpec((tm, k), lambda j, i: (s_blk0 + jnp.maximum(i - n_p, 0), 0)),
            pl.BlockSpec((None, k, tn), lambda j, i: (layer, 0, j)),
        ],
        out_specs=pl.BlockSpec((tm, tn), lambda j, i: (i, j)),
        out_shape=jax.ShapeDtypeStruct((mp + ms, n), out_dtype),
        scratch_shapes=[pltpu.VMEM((k, tn), BF16)],
        compiler_params=_params("arbitrary", "arbitrary"),
    )(xp, xs, w)


def _lora_body(xp_ref, xs_ref, w1_ref, w2_ref, o_ref, w1bf_ref, w2bf_ref, *, n_p, act):
    i = pl.program_id(0)

    @pl.when(i == 0)
    def _():
        w1bf_ref[...] = w1_ref[...].astype(BF16)
        w2bf_ref[...] = w2_ref[...].astype(BF16)

    def run(x_ref):
        t = jnp.dot(x_ref[...], w1bf_ref[...], preferred_element_type=F32)
        if act == "tanh":
            t = jnp.tanh(t)
        elif act == "sigmoid":
            t = _sigmoid(t)
        o_ref[...] = jnp.dot(t.astype(BF16), w2bf_ref[...], preferred_element_type=F32)

    @pl.when(i < n_p)
    def _():
        run(xp_ref)

    @pl.when(i >= n_p)
    def _():
        run(xs_ref)


def lora(act_in, w1, w2, mp, ms, act, tm_cap=256):
    xp, xs, s_off = _split_rows(act_in, mp)
    pad = -w1.shape[1] % LANES
    w1 = jnp.pad(w1, ((0, 0), (0, pad)))
    w2 = jnp.pad(w2, ((0, pad), (0, 0)))
    k, r = w1.shape
    n = w2.shape[1]
    tm = _row_tile(mp, ms, tm_cap)
    n_p, n_s = mp // tm, ms // tm
    s_blk0 = s_off // tm
    return pl.pallas_call(
        functools.partial(_lora_body, n_p=n_p, act=act),
        grid=(n_p + n_s,),
        in_specs=[
            pl.BlockSpec((tm, k), lambda i: (jnp.minimum(i, n_p - 1), 0)),
            pl.BlockSpec((tm, k), lambda i: (s_blk0 + jnp.maximum(i - n_p, 0), 0)),
            pl.BlockSpec((k, r), lambda i: (0, 0)),
            pl.BlockSpec((r, n), lambda i: (0, 0)),
        ],
        out_specs=pl.BlockSpec((tm, n), lambda i: (i, 0)),
        out_shape=jax.ShapeDtypeStruct((mp + ms, n), F32),
        scratch_shapes=[pltpu.VMEM((k, r), BF16), pltpu.VMEM((r, n), BF16)],
        compiler_params=_params("arbitrary"),
    )(xp, xs, w1, w2)


def _layer_norm_rows(z, g, b):
    mu = jnp.mean(z, axis=-1, keepdims=True)
    zc = z - mu
    var = jnp.mean(zc * zc, axis=-1, keepdims=True)
    return zc * lax.rsqrt(var + LN_EPS) * g + b


def _ln_body(x_ref, mix_ref, g_ref, b_ref, *rest, alpha, with_router):
    if with_router:
        wr_ref, br_ref, o_ref, obf_ref, lg_ref = rest
    else:
        o_ref, obf_ref = rest
    y = _layer_norm_rows(alpha * x_ref[...] + mix_ref[...], g_ref[...], b_ref[...])
    ybf = y.astype(BF16)
    o_ref[...] = y
    obf_ref[...] = ybf
    if with_router:
        lg_ref[...] = jnp.dot(ybf, wr_ref[...].astype(BF16), preferred_element_type=F32) + br_ref[...]


def residual_layer_norm(x, mix, g, b, alpha, router=None, tm=256):
    m, d = x.shape
    tm = _div_tile(m, tm)
    row = pl.BlockSpec((tm, d), lambda i: (i, 0))
    vec = pl.BlockSpec((1, d), lambda i: (0, 0))
    in_specs = [row, row, vec, vec]
    args = [x, mix, g.reshape(1, d), b.reshape(1, d)]
    out_specs = [row, row]
    out_shape = [jax.ShapeDtypeStruct((m, d), F32), jax.ShapeDtypeStruct((m, d), BF16)]
    if router is not None:
        wr, br = router
        in_specs += [pl.BlockSpec((d, ROUTE_PAD), lambda i: (0, 0)), pl.BlockSpec((1, ROUTE_PAD), lambda i: (0, 0))]
        args += [wr, br]
        out_specs.append(pl.BlockSpec((tm, ROUTE_PAD), lambda i: (i, 0)))
        out_shape.append(jax.ShapeDtypeStruct((m, ROUTE_PAD), F32))
    return pl.pallas_call(
        functools.partial(_ln_body, alpha=alpha, with_router=router is not None),
        grid=(m // tm,),
        in_specs=in_specs,
        out_specs=out_specs,
        out_shape=out_shape,
        compiler_params=_params("arbitrary"),
    )(*args)


def _lru_gates(xc, gate, wa_ref, wx_ref, ba_ref, bx_ref, lam_ref):
    xcb = xc.astype(BF16)
    r = _sigmoid(jnp.dot(xcb, wa_ref[0].astype(BF16), preferred_element_type=F32) + ba_ref[...])
    i = _sigmoid(jnp.dot(xcb, wx_ref[0].astype(BF16), preferred_element_type=F32) + bx_ref[...])
    log_a = -LRU_C * r * _softplus(-lam_ref[...])
    a = jnp.exp(log_a)
    mult = jnp.sqrt(1.0 - a * a)
    return a, mult * (i * xc), _gelu_tanh(gate)


def _lru_prompt_body(xb_ref, gate_ref, cw_ref, cb_ref, wa_ref, wx_ref, ba_ref, bx_ref, lam_ref,
                     y_ref, tail_ref, h_ref, prev_ref, hc_ref):
    c = pl.program_id(2)

    @pl.when(c == 0)
    def _():
        prev_ref[...] = jnp.zeros_like(prev_ref)
        hc_ref[...] = jnp.zeros_like(hc_ref)

    xb = xb_ref[...]
    tt = xb.shape[0]
    prev8 = prev_ref[...]
    row8 = lax.broadcasted_iota(jnp.int32, prev8.shape, 0)
    cw = cw_ref[...]
    xc = cb_ref[...] + xb * cw[CONV_W - 1:CONV_W]
    for j in range(1, CONV_W):
        rolled = pltpu.roll(xb, j, 0)
        head = jnp.where(row8 < j, pltpu.roll(prev8, j, 0), rolled[:SUBLANES])
        xj = jnp.concatenate([head, rolled[SUBLANES:]], axis=0)
        xc = xc + xj * cw[CONV_W - 1 - j:CONV_W - j]

    a, b, gg = _lru_gates(xc, gate_ref[...], wa_ref, wx_ref, ba_ref, bx_ref, lam_ref)

    row = lax.broadcasted_iota(jnp.int32, a.shape, 0)
    s = 1
    while s < tt:
        keep = row >= s
        b = jnp.where(keep, a * pltpu.roll(b, s, 0) + b, b)
        a = jnp.where(keep, a * pltpu.roll(a, s, 0), a)
        s *= 2
    h = a * hc_ref[...] + b

    y_ref[...] = (h * gg).astype(y_ref.dtype)
    tail_ref[0] = xb[tt - SUBLANES:]
    h_ref[0] = h[tt - 1:]
    prev_ref[...] = xb[tt - SUBLANES:]
    hc_ref[...] = h[tt - 1:]


def lru_prompt(proj, p, j, bsz, t_len, tt=256):
    d = proj.shape[1] // 2
    nb, bw = p['lru_w_a'].shape[1], p['lru_w_a'].shape[2]
    tt = min(tt, t_len)
    assert t_len % tt == 0 and tt % SUBLANES == 0
    nc = t_len // tt
    vec = lambda a: a[j].reshape(1, d)
    vspec = pl.BlockSpec((1, bw), lambda b, n, c: (0, n))
    wspec = pl.BlockSpec((1, bw, bw), lambda b, n, c: (n, 0, 0))
    return pl.pallas_call(
        _lru_prompt_body,
        grid=(bsz, nb, nc),
        in_specs=[
            pl.BlockSpec((tt, bw), lambda b, n, c: (b * nc + c, n)),
            pl.BlockSpec((tt, bw), lambda b, n, c: (b * nc + c, nb + n)),
            pl.BlockSpec((CONV_W, bw), lambda b, n, c: (0, n)),
            vspec, wspec, wspec, vspec, vspec, vspec,
        ],
        out_specs=[
            pl.BlockSpec((tt, bw), lambda b, n, c: (b * nc + c, n)),
            pl.BlockSpec((1, SUBLANES, bw), lambda b, n, c: (b, 0, n)),
            pl.BlockSpec((1, 1, bw), lambda b, n, c: (b, 0, n)),
        ],
        out_shape=[
            jax.ShapeDtypeStruct((bsz * t_len, d), BF16),
            jax.ShapeDtypeStruct((bsz, SUBLANES, d), F32),
            jax.ShapeDtypeStruct((bsz, 1, d), F32),
        ],
        scratch_shapes=[pltpu.VMEM((SUBLANES, bw), F32), pltpu.VMEM((1, bw), F32)],
        compiler_params=_params("arbitrary", "arbitrary", "arbitrary"),
    )(proj, proj, p['lru_conv_w'][j], vec(p['lru_conv_b']), p['lru_w_a'][j], p['lru_w_x'][j],
      vec(p['lru_b_a']), vec(p['lru_b_x']), vec(p['lru_lam']))


def _lru_sample_body(xb_ref, gate_ref, cst_ref, h0_ref, cw_ref, cb_ref, wa_ref, wx_ref, ba_ref, bx_ref,
                     lam_ref, y_ref, tail_ref, h_ref, *, t_len, nseq):
    cw = cw_ref[...]
    xs = [cst_ref[q] for q in range(CONV_W - 1)]
    xs += [xb_ref[t * nseq:(t + 1) * nseq] for t in range(t_len)]
    xcs = []
    for t in range(t_len):
        xc = cb_ref[...] + xs[t] * cw[0:1]
        for q in range(1, CONV_W):
            xc = xc + xs[t + q] * cw[q:q + 1]
        xcs.append(xc)
    a, b, gg = _lru_gates(jnp.concatenate(xcs, axis=0), gate_ref[...], wa_ref, wx_ref, ba_ref, bx_ref, lam_ref)
    h = h0_ref[...]
    for t in range(t_len):
        sl = slice(t * nseq, (t + 1) * nseq)
        h = a[sl] * h + b[sl]
        y_ref[sl, :] = (h * gg[sl]).astype(y_ref.dtype)
    h_ref[...] = h
    for q in range(CONV_W - 1):
        tail_ref[q] = xs[t_len + q]


def lru_sample(proj, conv_st, h0, p, j, mp, nseq, t_len):
    d = proj.shape[1] // 2
    nb, bw = p['lru_w_a'].shape[1], p['lru_w_a'].shape[2]
    ms = nseq * t_len
    assert mp % ms == 0
    blk0 = mp // ms
    vec = lambda a: a[j].reshape(1, d)
    vspec = pl.BlockSpec((1, bw), lambda n: (0, n))
    wspec = pl.BlockSpec((1, bw, bw), lambda n: (n, 0, 0))
    return pl.pallas_call(
        functools.partial(_lru_sample_body, t_len=t_len, nseq=nseq),
        grid=(nb,),
        in_specs=[
            pl.BlockSpec((ms, bw), lambda n: (blk0, n)),
            pl.BlockSpec((ms, bw), lambda n: (blk0, nb + n)),
            pl.BlockSpec((CONV_W - 1, nseq, bw), lambda n: (0, 0, n)),
            pl.BlockSpec((nseq, bw), lambda n: (0, n)),
            pl.BlockSpec((CONV_W, bw), lambda n: (0, n)),
            vspec, wspec, wspec, vspec, vspec, vspec,
        ],
        out_specs=[
            pl.BlockSpec((ms, bw), lambda n: (0, n)),
            pl.BlockSpec((CONV_W - 1, nseq, bw), lambda n: (0, 0, n)),
            pl.BlockSpec((nseq, bw), lambda n: (0, n)),
        ],
        out_shape=[
            jax.ShapeDtypeStruct((ms, d), BF16),
            jax.ShapeDtypeStruct((CONV_W - 1, nseq, d), F32),
            jax.ShapeDtypeStruct((nseq, d), F32),
        ],
        compiler_params=_params("arbitrary"),
    )(proj, proj, conv_st, h0, p['lru_conv_w'][j], vec(p['lru_conv_b']), p['lru_w_a'][j], p['lru_w_x'][j],
      vec(p['lru_b_a']), vec(p['lru_b_x']), vec(p['lru_lam']))


def _mix_store(x, prev, mu_ref, out_refs):
    xx = prev - x
    for q, o_ref in enumerate(out_refs):
        o_ref[...] = (x + xx * mu_ref[q:q + 1]).astype(o_ref.dtype)


def _shift_prompt_body(x_ref, mu_ref, *rest, blocks_per_seq):
    out_refs, carry_ref = rest[:-1], rest[-1]
    i = pl.program_id(1)
    x = x_ref[...]
    first = jnp.where(i % blocks_per_seq == 0, jnp.zeros_like(carry_ref[...]), carry_ref[...])
    row = lax.broadcasted_iota(jnp.int32, x.shape, 0)
    prev = jnp.where(row == 0, first, pltpu.roll(x, 1, 0))
    _mix_store(x, prev, mu_ref, out_refs)
    carry_ref[...] = x[x.shape[0] - 1:]


def shift_prompt(x, mu, mp, t_len, tm=256, tc=1024):
    d = x.shape[1]
    tm, tc = min(tm, t_len), min(tc, d)
    assert t_len % tm == 0 and d % tc == 0
    nq = mu.shape[0]
    spec = pl.BlockSpec((tm, tc), lambda c, i: (i, c))
    return pl.pallas_call(
        functools.partial(_shift_prompt_body, blocks_per_seq=t_len // tm),
        grid=(d // tc, mp // tm),
        in_specs=[spec, pl.BlockSpec((nq, tc), lambda c, i: (0, c))],
        out_specs=[spec] * nq,
        out_shape=[jax.ShapeDtypeStruct((mp, d), BF16)] * nq,
        scratch_shapes=[pltpu.VMEM((1, tc), F32)],
        compiler_params=_params("arbitrary", "arbitrary"),
    )(x, mu)


def _shift_sample_body(x_ref, st_ref, mu_ref, *out_refs, nseq):
    x = x_ref[...]
    prev = jnp.concatenate([st_ref[...], x[:x.shape[0] - nseq]], axis=0)
    _mix_store(x, prev, mu_ref, out_refs)


def shift_sample(x, state, mu, mp, nseq, t_len, tc=1024):
    d = x.shape[1]
    tc = min(tc, d)
    ms = nseq * t_len
    blk0 = mp // ms
    nq = mu.shape[0]
    return pl.pallas_call(
        functools.partial(_shift_sample_body, nseq=nseq),
        grid=(d // tc,),
        in_specs=[pl.BlockSpec((ms, tc), lambda c: (blk0, c)),
                  pl.BlockSpec((nseq, tc), lambda c: (0, c)),
                  pl.BlockSpec((nq, tc), lambda c: (0, c))],
        out_specs=[pl.BlockSpec((ms, tc), lambda c: (0, c))] * nq,
        out_shape=[jax.ShapeDtypeStruct((ms, d), BF16)] * nq,
        compiler_params=_params("arbitrary"),
    )(x, state, mu)


def _seg_sum(x):
    lo = lax.broadcasted_iota(jnp.int32, x.shape, x.ndim - 1) < RWKV_HEAD
    s_lo = jnp.sum(jnp.where(lo, x, 0.0), axis=-1, keepdims=True)
    s_hi = jnp.sum(jnp.where(lo, 0.0, x), axis=-1, keepdims=True)
    return jnp.where(lo, s_lo, s_hi)


def _wkv_parse(refs, has_vres, has_state, n_scratch):
    it = iter(refs)
    acts = tuple(next(it) for _ in range(6))
    vres = tuple(next(it) for _ in range(3)) if has_vres else None
    vecs = tuple(next(it) for _ in range(7))
    s0_ref = next(it) if has_state else None
    y_ref, s_ref = next(it), next(it)
    scratch = tuple(next(it) for _ in range(n_scratch))
    return acts, vres, vecs, s0_ref, y_ref, s_ref, scratch


def _wkv_operands(acts, vres, vecs, n_pairs):
    r_ref, k_ref, v_ref, wr_ref, ar_ref, _ = acts
    w0_ref, a0_ref, kk_ref, ka_ref, rk_ref, _, _ = vecs
    r = r_ref[...]
    k = k_ref[...]
    v = v_ref[...]
    w = -_softplus(-(w0_ref[...] + wr_ref[...])) - 0.5
    a = _sigmoid(a0_ref[...] + ar_ref[...])
    if vres is not None:
        vr_ref, vf_ref, v0_ref = vres
        v = v + (vf_ref[...] - v) * _sigmoid(v0_ref[...] + vr_ref[...])
    k2 = k * (1.0 + (a - 1.0) * ka_ref[...])
    kks = []
    for pr in range(n_pairs):
        ls = slice(pr * PAIR, (pr + 1) * PAIR)
        kkr = k[:, ls] * kk_ref[:, ls]
        kks.append(kkr / jnp.maximum(jnp.sqrt(_seg_sum(kkr * kkr)), 1e-12))
    kk = kks[0] if n_pairs == 1 else jnp.concatenate(kks, axis=1)
    return r, v, k2, kk, kk * a, -jnp.exp(w), r * k2 * rk_ref[...]


def _wkv_finish(ys_s, bonus_terms, v, acts, vecs, y_ref, n_pairs):
    g_ref, lg_ref, lb_ref = acts[5], vecs[5], vecs[6]
    for pr in range(n_pairs):
        ls = slice(pr * PAIR, (pr + 1) * PAIR)
        y = ys_s[:, ls]
        mu = _seg_sum(y) * (1.0 / RWKV_HEAD)
        yc = y - mu
        var = _seg_sum(yc * yc) * (1.0 / RWKV_HEAD)
        yn = yc * lax.rsqrt(var + GN_EPS) * lg_ref[:, ls] + lb_ref[:, ls]
        bonus = _seg_sum(bonus_terms[:, ls]) * v[:, ls]
        y_ref[:, ls] = ((yn + bonus) * g_ref[:, ls]).astype(y_ref.dtype)


def _wkv_steps_body(*refs, nb, nt, n_pairs, has_vres):
    acts, vres, vecs, s0_ref, y_ref, s_ref, scratch = _wkv_parse(refs, has_vres, True, 6)
    dec_s, kkn_s, kab_s, k2_s, v_s, ys_s = scratch
    r_ref = acts[0]
    s_ref[...] = s0_ref[...]
    _, v, k2, kk, kka, logdec, bonus_terms = _wkv_operands(acts, vres, vecs, n_pairs)
    dec_s[...] = jnp.exp(logdec)
    kkn_s[...] = -kk
    kab_s[...] = kka
    k2_s[...] = k2
    v_s[...] = v

    eye = (lax.broadcasted_iota(jnp.int32, (RWKV_HEAD, PAIR), 1) % RWKV_HEAD
           == lax.broadcasted_iota(jnp.int32, (RWKV_HEAD, PAIR), 0))
    sub8 = lax.broadcasted_iota(jnp.int32, (SUBLANES, PAIR), 0)
    operand_refs = (kkn_s, v_s, dec_s, kab_s, k2_s, r_ref)

    half_sum = ((lax.broadcasted_iota(jnp.int32, (2 * PAIR, PAIR), 0) % PAIR) // RWKV_HEAD
                == lax.broadcasted_iota(jnp.int32, (2 * PAIR, PAIR), 1) // RWKV_HEAD).astype(BF16)

    def seg_sums(xs):
        cat = jnp.concatenate(xs, axis=0)
        hi = cat.astype(BF16)
        lo = (cat - hi.astype(F32)).astype(BF16)
        out = jnp.dot(jnp.concatenate([hi, lo], axis=1), half_sum, preferred_element_type=F32)
        return [out[q * RWKV_HEAD:(q + 1) * RWKV_HEAD] for q in range(len(xs))]

    chains = range(SUBLANES)
    for t in range(nt):
        def seq_group(gi, carry):
            b0 = gi * SUBLANES
            rows = pl.ds(pl.multiple_of(t * nb + b0, SUBLANES), SUBLANES)
            for pr in range(n_pairs):
                ls = slice(pr * PAIR, (pr + 1) * PAIR)
                kkn, vv, dec, kab, k2r, rr = (ref[rows, ls] for ref in operand_refs)
                s = [s_ref[b0 + u, pr] for u in chains]
                sums = seg_sums([s[u] * kkn[u:u + 1] for u in chains]
                                + [jnp.where(eye, vv[u:u + 1], 0.0) for u in chains])
                s = [s[u] * dec[u:u + 1] + sums[u] * kab[u:u + 1] + sums[SUBLANES + u] * k2r[u:u + 1]
                     for u in chains]
                ycol = seg_sums([s[u] * rr[u:u + 1] for u in chains])
                ytile = jnp.zeros((SUBLANES, PAIR), F32)
                for u in chains:
                    s_ref[b0 + u, pr] = s[u]
                    yrow = jnp.sum(jnp.where(eye, ycol[u], 0.0), axis=0, keepdims=True)
                    ytile = jnp.where(sub8 == u, yrow, ytile)
                ys_s[rows, ls] = ytile
            return carry

        lax.fori_loop(0, nb // SUBLANES, seq_group, 0)

    _wkv_finish(ys_s, bonus_terms, v, acts, vecs, y_ref, n_pairs)


def _bdot(a, b, dims=(((1,), (0,)), ((), ()))):
    return lax.dot_general(a.astype(BF16), b.astype(BF16), dims, preferred_element_type=F32)


_NT = (((1,), (1,)), ((), ()))
_TN = (((0,), (0,)), ((), ()))


def _wkv_chunk_body(*refs, n_pairs, has_vres, chunk):
    acts, vres, vecs, _, y_ref, z_ref, scratch = _wkv_parse(refs, has_vres, False, 6)
    ld_s, kk_s, ka_s, k2_s, v_s, ys_s = scratch
    r_ref = acts[0]

    @pl.when(pl.program_id(2) == 0)
    def _():
        z_ref[...] = jnp.zeros_like(z_ref)

    _, v, k2, kk, kka, logdec, bonus_terms = _wkv_operands(acts, vres, vecs, n_pairs)
    ld_s[...] = logdec
    kk_s[...] = kk
    ka_s[...] = kka
    k2_s[...] = k2
    v_s[...] = v

    row_c = lax.broadcasted_iota(jnp.int32, (chunk, PAIR), 0)
    lo_c = lax.broadcasted_iota(jnp.int32, (chunk, PAIR), 1) < RWKV_HEAD
    i2 = lax.broadcasted_iota(jnp.int32, (2 * chunk, 2 * chunk), 0)
    j2 = lax.broadcasted_iota(jnp.int32, (2 * chunk, 2 * chunk), 1)
    same = (i2 // chunk) == (j2 // chunk)
    strict = same & ((i2 % chunk) > (j2 % chunk))
    incl = same & ((i2 % chunk) >= (j2 % chunk))
    ik = lax.broadcasted_iota(jnp.int32, (PAIR, PAIR), 0)
    jk = lax.broadcasted_iota(jnp.int32, (PAIR, PAIR), 1)
    n2 = 2 * chunk

    def stack2(x):
        return jnp.concatenate([jnp.where(lo_c, x, 0.0), jnp.where(lo_c, 0.0, x)], axis=0)

    def chunk_step(ci, carry):
        rows = pl.ds(pl.multiple_of(ci * chunk, chunk), chunk)
        pairs = range(n_pairs)
        lss = [slice(pr * PAIR, (pr + 1) * PAIR) for pr in pairs]
        qk, qr, kcat, kag, kkg, vb, g_col = ([] for _ in range(7))
        for ls in lss:
            ld = ld_s[rows, ls]
            cum = ld
            s = 1
            while s < chunk:
                cum = cum + jnp.where(row_c >= s, pltpu.roll(cum, s, 0), 0.0)
                s *= 2
            lwc = cum[chunk - 1:chunk]
            kk_c, ka_c, k2_c = kk_s[rows, ls], ka_s[rows, ls], k2_s[rows, ls]
            e_inv = jnp.exp(-cum)
            e_end = jnp.exp(lwc - cum)
            qk.append(stack2(kk_c * jnp.exp(cum - ld)).astype(BF16))
            qr.append(stack2(r_ref[rows, ls] * jnp.exp(cum)))
            kcat.append(jnp.concatenate([stack2(ka_c * e_inv), stack2(k2_c * e_inv)], axis=0).astype(BF16))
            kag.append(stack2(ka_c * e_end).astype(BF16))
            kkg.append(stack2(k2_c * e_end).astype(BF16))
            vb.append(stack2(v_s[rows, ls]).astype(BF16))
            g_col.append(jnp.sum(jnp.where(ik == jk, jnp.exp(lwc), 0.0), axis=1, keepdims=True))
        sc = [_bdot(jnp.concatenate([qk[p], qr[p].astype(BF16)], axis=0), kcat[p], _NT) for p in pairs]
        a_ak = [jnp.where(strict, sc[p][:n2, :n2], 0.0) for p in pairs]
        a_kk = [jnp.where(strict, sc[p][:n2, n2:], 0.0).astype(BF16) for p in pairs]
        b_ak = [jnp.where(incl, sc[p][n2:, :n2], 0.0).astype(BF16) for p in pairs]
        b_kk = [jnp.where(incl, sc[p][n2:, n2:], 0.0).astype(BF16) for p in pairs]
        tinv = [jnp.where(i2 == j2, 1.0, 0.0) - a for a in a_ak]
        apow = [a.astype(BF16) for a in a_ak]
        s = 1
        while 2 * s < chunk:
            apow = [_bdot(a, a).astype(BF16) for a in apow]
            tinv = [t + _bdot(t, a) for t, a in zip(tinv, apow)]
            s *= 2
        w = [_bdot(a_kk[p], vb[p]).astype(BF16) for p in pairs]
        pu = [(-_bdot(tinv[p], jnp.concatenate([qk[p], w[p]], axis=1))).astype(BF16) for p in pairs]
        bp = [_bdot(b_ak[p], pu[p]) for p in pairs]
        kp = [_bdot(kag[p], pu[p], _TN) for p in pairs]
        y0 = [bp[p][:, PAIR:] + _bdot(b_kk[p], vb[p]) for p in pairs]
        z0 = [kp[p][:, PAIR:] + _bdot(kkg[p], vb[p], _TN) for p in pairs]
        for p in pairs:
            z = z_ref[0, p]
            zb = z.astype(BF16)
            yb = _bdot(qr[p] + bp[p][:, :PAIR], zb) + y0[p]
            ys_s[rows, lss[p]] = yb[:chunk] + yb[chunk:]
            z_ref[0, p] = g_col[p] * z + _bdot(kp[p][:, :PAIR], zb) + z0[p]
        return carry

    lax.fori_loop(0, r_ref.shape[0] // chunk, chunk_step, 0)
    _wkv_finish(ys_s, bonus_terms, v, acts, vecs, y_ref, n_pairs)


def _wkv_call(body, acts, vres, p, j, state, *, grid, rows, n_pairs, act_map, out_map, vec_map, st_map,
              out_rows, state_shape, state_block):
    d = acts[0].shape[1]
    lw = n_pairs * PAIR
    aspec = pl.BlockSpec((rows, lw), act_map)
    vspec = pl.BlockSpec((1, lw), vec_map)
    sspec = pl.BlockSpec(state_block, st_map)
    vec = lambda a: a.reshape(1, d)
    args, in_specs = list(acts), [aspec] * 6
    if vres is not None:
        v_raw, v_first, v0 = vres
        args += [v_raw, v_first, vec(v0)]
        in_specs += [aspec, aspec, vspec]
    args += [vec(p['rw_w0'][j]), vec(p['rw_a0'][j]), vec(p['rw_k_k'][j]), vec(p['rw_k_a'][j]),
             vec(p['rw_r_k'][j]), vec(p['rw_lnx_g'][j]), vec(p['rw_lnx_b'][j])]
    in_specs += [vspec] * 7
    if state is not None:
        args.append(state)
        in_specs.append(sspec)
    return pl.pallas_call(
        body,
        grid=grid,
        in_specs=in_specs,
        out_specs=[pl.BlockSpec((rows, lw), out_map), sspec],
        out_shape=[jax.ShapeDtypeStruct((out_rows, d), BF16), jax.ShapeDtypeStruct(state_shape, F32)],
        scratch_shapes=[pltpu.VMEM((rows, lw), F32)] * 6,
        compiler_params=_params(*(["arbitrary"] * len(grid))),
    )(*args)


def wkv_prompt(acts, vres, p, j, bsz, t_len, n_pairs=8, rows=256, chunk=64):
    d = acts[0].shape[1]
    n_pairs = min(n_pairs, d // PAIR)
    rows = min(rows, t_len)
    assert t_len % rows == 0 and rows % chunk == 0
    nc = t_len // rows
    return _wkv_call(
        functools.partial(_wkv_chunk_body, n_pairs=n_pairs, has_vres=vres is not None, chunk=chunk),
        acts, vres, p, j, None, grid=(bsz, d // (n_pairs * PAIR), nc), rows=rows, n_pairs=n_pairs,
        act_map=lambda b, l, c: (b * nc + c, l), out_map=lambda b, l, c: (b * nc + c, l),
        vec_map=lambda b, l, c: (0, l), st_map=lambda b, l, c: (b, l, 0, 0),
        out_rows=bsz * t_len, state_shape=(bsz, d // PAIR, PAIR, PAIR), state_block=(1, n_pairs, PAIR, PAIR))


def wkv_sample(acts, vres, p, j, state, row_blk0, nseq, t_len):
    d = acts[0].shape[1]
    return _wkv_call(
        functools.partial(_wkv_steps_body, nb=nseq, nt=t_len, n_pairs=1, has_vres=vres is not None),
        acts, vres, p, j, state, grid=(d // PAIR,), rows=nseq * t_len, n_pairs=1,
        act_map=lambda l: (row_blk0, l), out_map=lambda l: (0, l), vec_map=lambda l: (0, l),
        st_map=lambda l: (0, l, 0, 0), out_rows=nseq * t_len, state_shape=(nseq, d // PAIR, RWKV_HEAD, PAIR),
        state_block=(nseq, 1, RWKV_HEAD, PAIR))


def _pair_state(s):
    b, h, nv, nk = s.shape
    return s.reshape(b, h // 2, 2, nv, nk).transpose(0, 1, 3, 2, 4).reshape(b, h // 2, nv, 2 * nk)


def _unpair_state(s):
    b, hp, nv, nk2 = s.shape
    return s.reshape(b, hp, nv, 2, nk2 // 2).transpose(0, 1, 3, 2, 4).reshape(b, hp * 2, nv, nk2 // 2)


def _state_from_blockdiag(z):
    b, hp = z.shape[:2]
    z = z.reshape(b, hp, 2, RWKV_HEAD, 2, RWKV_HEAD)
    diag = jnp.stack([z[:, :, 0, :, 0, :], z[:, :, 1, :, 1, :]], axis=2)
    return diag.transpose(0, 1, 2, 4, 3).reshape(b, hp * 2, RWKV_HEAD, RWKV_HEAD)


def _route(logits, n_groups, n_experts, tm):
    m = logits.shape[0]
    epg = n_experts // n_groups
    g_logits = logits[:, :n_groups]
    g_idx = jnp.argmax(g_logits, axis=-1)
    p_group = jnp.max(jax.nn.softmax(g_logits, axis=-1), axis=-1, keepdims=True)
    e_logits = logits[:, n_groups:n_groups + n_experts].reshape(m, n_groups, epg)
    e_in = jnp.take_along_axis(e_logits, g_idx[:, None, None], axis=1)[:, 0]
    top_val, top_idx = lax.top_k(e_in, TOP_K)
    gate = p_group * jax.nn.softmax(top_val, axis=-1)
    expert_flat = (g_idx[:, None] * epg + top_idx).astype(jnp.int32).reshape(-1)

    n_assign = m * TOP_K
    order = jnp.argsort(expert_flat).astype(jnp.int32)
    e_sorted = expert_flat[order]
    tok_sorted = order // TOP_K
    counts = jnp.sum(expert_flat[:, None] == jnp.arange(n_experts, dtype=jnp.int32)[None, :], axis=0, dtype=jnp.int32)
    start = jnp.cumsum(counts) - counts
    nblk_e = (counts + tm - 1) // tm
    blk_end = jnp.cumsum(nblk_e)
    blk_start = blk_end - nblk_e
    slot_sorted = blk_start[e_sorted] * tm + (jnp.arange(n_assign, dtype=jnp.int32) - start[e_sorted])
    slot_of_assign = slot_sorted[jnp.argsort(order)]
    n_blocks = (n_assign + n_experts * (tm - 1) + tm - 1) // tm
    blk = jnp.arange(n_blocks, dtype=jnp.int32)
    blk_expert = jnp.minimum(jnp.sum(blk[:, None] >= blk_end[None, :], axis=1), n_experts - 1)
    e_slot = jnp.repeat(blk_expert, tm)
    off = jnp.arange(n_blocks * tm, dtype=jnp.int32) - blk_start[e_slot] * tm
    valid = (off >= 0) & (off < counts[e_slot])
    slot_tok = jnp.where(valid, tok_sorted[jnp.clip(start[e_slot] + off, 0, n_assign - 1)], 0)
    return slot_tok, slot_of_assign.reshape(m, TOP_K), gate.astype(F32), blk_start, nblk_e


def _gather_body(tok_ref, x_hbm, o_ref, buf_ref, sem, *, tm):
    i = pl.program_id(0)

    def copy(step, r):
        slot = step % 2
        return pltpu.make_async_copy(x_hbm.at[pl.ds(tok_ref[step * tm + r], 1)],
                                     buf_ref.at[slot, pl.ds(r, 1)], sem.at[slot])

    def issue(step):
        lax.fori_loop(0, tm, lambda r, c: (copy(step, r).start(), c)[1], 0)

    @pl.when(i == 0)
    def _():
        issue(0)

    @pl.when(i + 1 < pl.num_programs(0))
    def _():
        issue(i + 1)

    lax.fori_loop(0, tm, lambda r, c: (copy(i, r).wait(), c)[1], 0)
    o_ref[...] = buf_ref[i % 2].astype(o_ref.dtype)


def gather_rows(x, slot_tok, tm=256):
    d = x.shape[1]
    n_slots = slot_tok.shape[0]
    tm = _div_tile(n_slots, tm)
    return pl.pallas_call(
        functools.partial(_gather_body, tm=tm),
        grid_spec=pltpu.PrefetchScalarGridSpec(
            num_scalar_prefetch=1,
            grid=(n_slots // tm,),
            in_specs=[pl.BlockSpec(memory_space=pl.ANY)],
            out_specs=pl.BlockSpec((tm, d), lambda i, tok: (i, 0)),
            scratch_shapes=[pltpu.VMEM((2, tm, d), F32), pltpu.SemaphoreType.DMA((2,))],
        ),
        out_shape=jax.ShapeDtypeStruct((n_slots, d), BF16),
        compiler_params=_params("arbitrary"),
    )(slot_tok, x)


def _expert_blocks(bs_ref, nb_ref, src_hbm, dst_hbm, xbuf, obuf, xsem, osem, compute, *, tm, dst_cols):
    e = pl.program_id(0)
    row0 = bs_ref[e] * tm
    n = nb_ref[e]

    def rows(b):
        return pl.ds(pl.multiple_of(row0 + b * tm, tm), tm)

    def xcopy(b):
        return pltpu.make_async_copy(src_hbm.at[rows(b)], xbuf.at[b % 2], xsem.at[b % 2])

    def ocopy(b):
        dst = dst_hbm.at[rows(b)] if dst_cols is None else dst_hbm.at[rows(b), dst_cols]
        return pltpu.make_async_copy(obuf.at[b % 2], dst, osem.at[b % 2])

    @pl.when(n > 0)
    def _():
        xcopy(0).start()

    def block(b, c):
        xcopy(b).wait()

        @pl.when(b + 1 < n)
        def _():
            xcopy(b + 1).start()

        @pl.when(b >= 2)
        def _():
            ocopy(b - 2).wait()

        obuf[b % 2] = compute(xbuf[b % 2]).astype(obuf.dtype)
        ocopy(b).start()
        return c

    lax.fori_loop(0, n, block, 0)

    @pl.when(n >= 2)
    def _():
        ocopy(n - 2).wait()

    @pl.when(n >= 1)
    def _():
        ocopy(n - 1).wait()

    @pl.when(e == pl.num_programs(0) - 1)
    def _():
        obuf[0] = jnp.zeros(obuf.shape[1:], obuf.dtype)

        def fill(blk, c):
            fill_rows = pl.ds(pl.multiple_of(blk * tm, tm), tm)
            dst = dst_hbm.at[fill_rows] if dst_cols is None else dst_hbm.at[fill_rows, dst_cols]
            cp = pltpu.make_async_copy(obuf.at[0], dst, osem.at[0])
            cp.start()
            cp.wait()
            return c

        lax.fori_loop(bs_ref[e] + n, dst_hbm.shape[0] // tm, fill, 0)


def _ffn_up_body(bs_ref, nb_ref, x_hbm, wg_ref, wu_ref, o_hbm, wgbf_ref, wubf_ref, xbuf, obuf, xsem, osem, *, tm, tn):
    wgbf_ref[...] = wg_ref[...].astype(BF16)
    wubf_ref[...] = wu_ref[...].astype(BF16)

    def compute(x):
        hg = jnp.dot(x, wgbf_ref[...], preferred_element_type=F32)
        hu = jnp.dot(x, wubf_ref[...], preferred_element_type=F32)
        return hg * _sigmoid(hg) * hu

    cols = pl.ds(pl.multiple_of(pl.program_id(1) * tn, tn), tn)
    _expert_blocks(bs_ref, nb_ref, x_hbm, o_hbm, xbuf, obuf, xsem, osem, compute, tm=tm, dst_cols=cols)


def _ffn_down_body(bs_ref, nb_ref, h_hbm, wd_ref, o_hbm, wdbf_ref, xbuf, obuf, xsem, osem, *, tm):
    wdbf_ref[...] = wd_ref[...].astype(BF16)

    def compute(h):
        return jnp.dot(h, wdbf_ref[...], preferred_element_type=F32)

    _expert_blocks(bs_ref, nb_ref, h_hbm, o_hbm, xbuf, obuf, xsem, osem, compute, tm=tm, dst_cols=None)


def expert_ffn(x_sorted, blk_start, nblk_e, w_gate, w_up, w_down, layer, tm, tn_up=512):
    n_slots, d = x_sorted.shape
    n_experts, de = w_gate.shape[1], w_gate.shape[3]
    tn_up = min(tn_up, de)
    any_spec = pl.BlockSpec(memory_space=pl.ANY)
    dma2 = pltpu.SemaphoreType.DMA((2,))
    hid = pl.pallas_call(
        functools.partial(_ffn_up_body, tm=tm, tn=tn_up),
        grid_spec=pltpu.PrefetchScalarGridSpec(
            num_scalar_prefetch=2,
            grid=(n_experts, de // tn_up),
            in_specs=[any_spec,
                      pl.BlockSpec((None, None, d, tn_up), lambda e, j, bs, nb: (layer, e, 0, j)),
                      pl.BlockSpec((None, None, d, tn_up), lambda e, j, bs, nb: (layer, e, 0, j))],
            out_specs=any_spec,
            scratch_shapes=[pltpu.VMEM((d, tn_up), BF16), pltpu.VMEM((d, tn_up), BF16),
                            pltpu.VMEM((2, tm, d), BF16), pltpu.VMEM((2, tm, tn_up), BF16), dma2, dma2],
        ),
        out_shape=jax.ShapeDtypeStruct((n_slots, de), BF16),
        compiler_params=_params("arbitrary", "arbitrary"),
    )(blk_start, nblk_e, x_sorted, w_gate, w_up)
    return pl.pallas_call(
        functools.partial(_ffn_down_body, tm=tm),
        grid_spec=pltpu.PrefetchScalarGridSpec(
            num_scalar_prefetch=2,
            grid=(n_experts,),
            in_specs=[any_spec, pl.BlockSpec((None, None, de, d), lambda e, bs, nb: (layer, e, 0, 0))],
            out_specs=any_spec,
            scratch_shapes=[pltpu.VMEM((de, d), BF16), pltpu.VMEM((2, tm, de), BF16),
                            pltpu.VMEM((2, tm, d), F32), dma2, dma2],
        ),
        out_shape=jax.ShapeDtypeStruct((n_slots, d), F32),
        compiler_params=_params("arbitrary"),
    )(blk_start, nblk_e, hid, w_down)


def _combine_ln_body(slot_ref, x_ref, gate_ref, g_ref, b_ref, y_hbm, o_ref, obf_ref, buf_ref, sem, *, tm, alpha):
    i = pl.program_id(0)

    def copy(step, q):
        slot = step % 2
        return pltpu.make_async_copy(y_hbm.at[pl.ds(slot_ref[step * tm * TOP_K + q], 1)],
                                     buf_ref.at[slot, q % TOP_K, pl.ds(q // TOP_K, 1)], sem.at[slot])

    def issue(step):
        lax.fori_loop(0, tm * TOP_K, lambda q, c: (copy(step, q).start(), c)[1], 0)

    @pl.when(i == 0)
    def _():
        issue(0)

    @pl.when(i + 1 < pl.num_programs(0))
    def _():
        issue(i + 1)

    lax.fori_loop(0, tm * TOP_K, lambda q, c: (copy(i, q).wait(), c)[1], 0)
    gate = gate_ref[...]
    buf = buf_ref.at[i % 2]
    ffn = buf[0] * gate[:, 0:1]
    for q in range(1, TOP_K):
        ffn = ffn + buf[q] * gate[:, q:q + 1]
    y = _layer_norm_rows(alpha * x_ref[...] + ffn, g_ref[...], b_ref[...])
    o_ref[...] = y
    obf_ref[...] = y.astype(BF16)


def combine_layer_norm(x, out_slots, slot_of_assign, gate, g, b, alpha, tm=256):
    m, d = x.shape
    tm = _div_tile(m, tm)
    row = pl.BlockSpec((tm, d), lambda i, s: (i, 0))
    vec = pl.BlockSpec((1, d), lambda i, s: (0, 0))
    return pl.pallas_call(
        functools.partial(_combine_ln_body, tm=tm, alpha=alpha),
        grid_spec=pltpu.PrefetchScalarGridSpec(
            num_scalar_prefetch=1,
            grid=(m // tm,),
            in_specs=[row, pl.BlockSpec((tm, TOP_K), lambda i, s: (i, 0)), vec, vec,
                      pl.BlockSpec(memory_space=pl.ANY)],
            out_specs=[row, row],
            scratch_shapes=[pltpu.VMEM((2, TOP_K, tm, d), F32), pltpu.SemaphoreType.DMA((2,))],
        ),
        out_shape=[jax.ShapeDtypeStruct((m, d), F32), jax.ShapeDtypeStruct((m, d), BF16)],
        compiler_params=_params("arbitrary"),
    )(slot_of_assign.reshape(-1), x, gate, g.reshape(1, d), b.reshape(1, d), out_slots)


def _trunk(x_prompt, x_sample, state_conv, state_h, state_shift, state_wkv, p, moe_tm=128):
    bsz, t_len, d = x_prompt.shape
    nseq, dec_t, _ = x_sample.shape
    mp, ms = bsz * t_len, nseq * dec_t
    depth = p['ln_g'].shape[0]
    alpha = (2 * depth) ** 0.25
    n_groups = p['moe_w_group'].shape[2]
    n_experts = p['moe_w_expert'].shape[2]

    x = jnp.concatenate([x_prompt.reshape(mp, d), x_sample.transpose(1, 0, 2).reshape(ms, d)], axis=0)
    xbf = x.astype(BF16)
    conv_s_tm = state_conv.transpose(0, 2, 1, 3)

    conv_p, h_p, shift_p, wkv_p, conv_s, h_s, shift_s, wkv_s = ([] for _ in range(8))
    v_first = None
    for i in range(depth):
        j = i // 2
        if i % 2 == 0:
            proj = matmul(xbf, p['lru_w_in'], j, mp, ms)
            y_p, tail_p, hl_p = lru_prompt(proj, p, j, bsz, t_len)
            y_s, tail_s, hl_s = lru_sample(proj, conv_s_tm[j], state_h[j], p, j, mp, nseq, dec_t)
            mix = matmul((y_p, y_s), p['lru_w_out'], j, mp, ms)
            conv_p.append(tail_p[:, SUBLANES - (CONV_W - 1):])
            h_p.append(hl_p[:, 0])
            conv_s.append(tail_s.transpose(1, 0, 2))
            h_s.append(hl_s)
        else:
            mu = p['rw_mu'][j]
            mixed_p = shift_prompt(x, mu, mp, t_len)
            mixed_s = shift_sample(x, state_shift[j], mu, mp, nseq, dec_t)
            xr, xw, xk, xv, xa, xg = zip(mixed_p, mixed_s)
            r = matmul(xr, p['rw_w_r'], j, mp, ms)
            k = matmul(xk, p['rw_w_k'], j, mp, ms)
            v = matmul(xv, p['rw_w_v'], j, mp, ms)
            w_raw = lora(xw, p['rw_w1'][j], p['rw_w2'][j], mp, ms, "tanh")
            a_raw = lora(xa, p['rw_a1'][j], p['rw_a2'][j], mp, ms, "none")
            g = lora(xg, p['rw_g1'][j], p['rw_g2'][j], mp, ms, "sigmoid")
            if j == 0:
                vres, v_first = None, v
            else:
                v_raw = lora(xv, p['rw_v1'][j - 1], p['rw_v2'][j - 1], mp, ms, "none")
                vres = (v_raw, v_first, p['rw_v0'][j - 1])
            acts = (r, k, v, w_raw, a_raw, g)
            yg_p, z_p = wkv_prompt(acts, vres, p, j, bsz, t_len)
            yg_s, st_s = wkv_sample(acts, vres, p, j, _pair_state(state_wkv[j]), mp // ms, nseq, dec_t)
            mix = matmul((yg_p, yg_s), p['rw_w_o'], j, mp, ms)
            shift_p.append(x[t_len - 1:mp:t_len])
            shift_s.append(x[mp + (dec_t - 1) * nseq:])
            wkv_p.append(_state_from_blockdiag(z_p))
            wkv_s.append(_unpair_state(st_s))

        w_route = jnp.concatenate([p['moe_w_group'][i], p['moe_w_expert'][i]], axis=1)
        b_route = jnp.concatenate([p['moe_b_group'][i], p['moe_b_expert'][i]], axis=0)
        pad = ROUTE_PAD - w_route.shape[1]
        w_route = jnp.pad(w_route, ((0, 0), (0, pad)))
        b_route = jnp.pad(b_route, (0, pad)).reshape(1, ROUTE_PAD)
        x, xbf, logits = residual_layer_norm(x, mix, p['ln_g'][i, 0], p['ln_b'][i, 0], alpha,
                                             router=(w_route, b_route))
        slot_tok, slot_of_assign, gate, blk_start, nblk_e = _route(logits, n_groups, n_experts, moe_tm)
        x_sorted = gather_rows(x, slot_tok)
        out_slots = expert_ffn(x_sorted, blk_start, nblk_e, p['moe_w_gate'], p['moe_w_up'], p['moe_w_down'],
                               i, moe_tm)
        x, xbf = combine_layer_norm(x, out_slots, slot_of_assign, gate, p['ln_g'][i, 1], p['ln_b'][i, 1], alpha)

    y_prompt = x[:mp].reshape(bsz, t_len, d)
    y_sample = x[mp:].reshape(dec_t, nseq, d).transpose(1, 0, 2)
    return (y_prompt, y_sample, jnp.stack(conv_p), jnp.stack(h_p), jnp.stack(shift_p), jnp.stack(wkv_p),
            jnp.stack(conv_s), jnp.stack(h_s), jnp.stack(shift_s), jnp.stack(wkv_s))


def kernel(x_prompt, x_sample, state_conv, state_h, state_shift, state_wkv, ln_g, ln_b, lru_w_in, lru_conv_w, lru_conv_b, lru_w_a, lru_b_a, lru_w_x, lru_b_x, lru_lam, lru_w_out, rw_mu, rw_w_r, rw_w_k, rw_w_v, rw_w_o, rw_w0, rw_w1, rw_w2, rw_a0, rw_a1, rw_a2, rw_v0, rw_v1, rw_v2, rw_g1, rw_g2, rw_k_k, rw_k_a, rw_r_k, rw_lnx_g, rw_lnx_b, moe_w_group, moe_b_group, moe_w_expert, moe_b_expert, moe_w_gate, moe_w_up, moe_w_down):
    p = dict(ln_g=ln_g, ln_b=ln_b, lru_w_in=lru_w_in, lru_conv_w=lru_conv_w, lru_conv_b=lru_conv_b,
             lru_w_a=lru_w_a, lru_b_a=lru_b_a, lru_w_x=lru_w_x, lru_b_x=lru_b_x, lru_lam=lru_lam,
             lru_w_out=lru_w_out, rw_mu=rw_mu, rw_w_r=rw_w_r, rw_w_k=rw_w_k, rw_w_v=rw_w_v, rw_w_o=rw_w_o,
             rw_w0=rw_w0, rw_w1=rw_w1, rw_w2=rw_w2, rw_a0=rw_a0, rw_a1=rw_a1, rw_a2=rw_a2,
             rw_v0=rw_v0, rw_v1=rw_v1, rw_v2=rw_v2, rw_g1=rw_g1, rw_g2=rw_g2, rw_k_k=rw_k_k, rw_k_a=rw_k_a,
             rw_r_k=rw_r_k, rw_lnx_g=rw_lnx_g, rw_lnx_b=rw_lnx_b, moe_w_group=moe_w_group,
             moe_b_group=moe_b_group, moe_w_expert=moe_w_expert, moe_b_expert=moe_b_expert,
             moe_w_gate=moe_w_gate, moe_w_up=moe_w_up, moe_w_down=moe_w_down)
    return _trunk(x_prompt, x_sample, state_conv, state_h, state_shift, state_wkv, p)
```

```python
import functools

import jax
import jax.numpy as jnp
from jax import lax
from jax.experimental import pallas as pl
from jax.experimental.pallas import tpu as pltpu

F32 = jnp.float32
BF16 = jnp.bfloat16

V7X_VMEM_LIMIT_BYTES = 56 * 1024 * 1024
LANES = 128
SUBLANES = 8
RWKV_HEAD = 64
PAIR = 2 * RWKV_HEAD
CONV_W = 4
LRU_C = 8.0
GN_EPS = 64e-5
LN_EPS = 1e-5
TOP_K = 2
ROUTE_PAD = LANES
BLOCK_DMA_PRIORITY = 1


def _params(*sem):
    return pltpu.CompilerParams(dimension_semantics=sem, vmem_limit_bytes=V7X_VMEM_LIMIT_BYTES)


def _sigmoid(x):
    return 1.0 / (1.0 + jnp.exp(-x))


def _softplus(x):
    return jnp.maximum(x, 0.0) + jnp.log1p(jnp.exp(-jnp.abs(x)))


def _gelu_tanh(x):
    return 0.5 * x * (1.0 + jnp.tanh(0.7978845608028654 * (x + 0.044715 * (x * x * x))))


def _row_tile(mp, ms, cap):
    tm = min(cap, ms)
    assert mp % tm == 0 and ms % tm == 0 and tm % SUBLANES == 0, (mp, ms, tm)
    return tm


def _div_tile(m, cap):
    tm = min(cap, m) // SUBLANES * SUBLANES
    while m % tm:
        tm -= SUBLANES
    return tm


def _mm_body(xp_ref, xs_ref, w_ref, o_ref, wbf_ref, *, n_p):
    i = pl.program_id(1)

    @pl.when(i == 0)
    def _():
        wbf_ref[...] = w_ref[...].astype(BF16)

    def run(x_ref):
        o_ref[...] = jnp.dot(x_ref[...], wbf_ref[...], preferred_element_type=F32).astype(o_ref.dtype)

    @pl.when(i < n_p)
    def _():
        run(xp_ref)

    @pl.when(i >= n_p)
    def _():
        run(xs_ref)


def _split_rows(act, mp):
    if isinstance(act, tuple):
        return act[0], act[1], 0
    return act, act, mp


def matmul(act, w, layer, mp, ms, out_dtype=F32, tn=512, tm_cap=512):
    xp, xs, s_off = _split_rows(act, mp)
    _, k, n = w.shape
    tn = min(tn, n)
    assert n % tn == 0
    tm = _row_tile(mp, ms, tm_cap)
    n_p, n_s = mp // tm, ms // tm
    s_blk0 = s_off // tm
    return pl.pallas_call(
        functools.partial(_mm_body, n_p=n_p),
        grid=(n // tn, n_p + n_s),
        in_specs=[
            pl.BlockSpec((tm, k), lambda j, i: (jnp.minimum(i, n_p - 1), 0)),
            pl.BlockSpec((tm, k), lambda j, i: (s_blk0 + jnp.maximum(i - n_p, 0), 0)),
            pl.BlockSpec((None, k, tn), lambda j, i: (layer, 0, j)),
        ],
        out_specs=pl.BlockSpec((tm, tn), lambda j, i: (i, j)),
        out_shape=jax.ShapeDtypeStruct((mp + ms, n), out_dtype),
        scratch_shapes=[pltpu.VMEM((k, tn), BF16)],
        compiler_params=_params("arbitrary", "arbitrary"),
    )(xp, xs, w)


def _lora_body(xp_ref, xs_ref, w1_ref, w2_ref, o_ref, w1bf_ref, w2bf_ref, *, n_p, act):
    i = pl.program_id(0)

    @pl.when(i == 0)
    def _():
        w1bf_ref[...] = w1_ref[...].astype(BF16)
        w2bf_ref[...] = w2_ref[...].astype(BF16)

    def run(x_ref):
        t = jnp.dot(x_ref[...], w1bf_ref[...], preferred_element_type=F32)
        if act == "tanh":
            t = jnp.tanh(t)
        elif act == "sigmoid":
            t = _sigmoid(t)
        o_ref[...] = jnp.dot(t.astype(BF16), w2bf_ref[...], preferred_element_type=F32)

    @pl.when(i < n_p)
    def _():
        run(xp_ref)

    @pl.when(i >= n_p)
    def _():
        run(xs_ref)


def lora(act_in, w1, w2, mp, ms, act, tm_cap=256):
    xp, xs, s_off = _split_rows(act_in, mp)
    pad = -w1.shape[1] % LANES
    w1 = jnp.pad(w1, ((0, 0), (0, pad)))
    w2 = jnp.pad(w2, ((0, pad), (0, 0)))
    k, r = w1.shape
    n = w2.shape[1]
    tm = _row_tile(mp, ms, tm_cap)
    n_p, n_s = mp // tm, ms // tm
    s_blk0 = s_off // tm
    return pl.pallas_call(
        functools.partial(_lora_body, n_p=n_p, act=act),
        grid=(n_p + n_s,),
        in_specs=[
            pl.BlockSpec((tm, k), lambda i: (jnp.minimum(i, n_p - 1), 0)),
            pl.BlockSpec((tm, k), lambda i: (s_blk0 + jnp.maximum(i - n_p, 0), 0)),
            pl.BlockSpec((k, r), lambda i: (0, 0)),
            pl.BlockSpec((r, n), lambda i: (0, 0)),
        ],
        out_specs=pl.BlockSpec((tm, n), lambda i: (i, 0)),
        out_shape=jax.ShapeDtypeStruct((mp + ms, n), F32),
        scratch_shapes=[pltpu.VMEM((k, r), BF16), pltpu.VMEM((r, n), BF16)],
        compiler_params=_params("arbitrary"),
    )(xp, xs, w1, w2)


def _layer_norm_rows(z, g, b):
    mu = jnp.mean(z, axis=-1, keepdims=True)
    zc = z - mu
    var = jnp.mean(zc * zc, axis=-1, keepdims=True)
    return zc * lax.rsqrt(var + LN_EPS) * g + b


def _ln_body(x_ref, mix_ref, g_ref, b_ref, *rest, alpha, with_router):
    if with_router:
        wr_ref, br_ref, o_ref, obf_ref, lg_ref = rest
    else:
        o_ref, obf_ref = rest
    y = _layer_norm_rows(alpha * x_ref[...] + mix_ref[...], g_ref[...], b_ref[...])
    ybf = y.astype(BF16)
    o_ref[...] = y
    obf_ref[...] = ybf
    if with_router:
        lg_ref[...] = jnp.dot(ybf, wr_ref[...].astype(BF16), preferred_element_type=F32) + br_ref[...]


def residual_layer_norm(x, mix, g, b, alpha, router=None, tm=256):
    m, d = x.shape
    tm = _div_tile(m, tm)
    row = pl.BlockSpec((tm, d), lambda i: (i, 0))
    vec = pl.BlockSpec((1, d), lambda i: (0, 0))
    in_specs = [row, row, vec, vec]
    args = [x, mix, g.reshape(1, d), b.reshape(1, d)]
    out_specs = [row, row]
    out_shape = [jax.ShapeDtypeStruct((m, d), F32), jax.ShapeDtypeStruct((m, d), BF16)]
    if router is not None:
        wr, br = router
        in_specs += [pl.BlockSpec((d, ROUTE_PAD), lambda i: (0, 0)), pl.BlockSpec((1, ROUTE_PAD), lambda i: (0, 0))]
        args += [wr, br]
        out_specs.append(pl.BlockSpec((tm, ROUTE_PAD), lambda i: (i, 0)))
        out_shape.append(jax.ShapeDtypeStruct((m, ROUTE_PAD), F32))
    return pl.pallas_call(
        functools.partial(_ln_body, alpha=alpha, with_router=router is not None),
        grid=(m // tm,),
        in_specs=in_specs,
        out_specs=out_specs,
        out_shape=out_shape,
        compiler_params=_params("arbitrary"),
    )(*args)


def _lru_gates(xc, gate, wa_ref, wx_ref, ba_ref, bx_ref, lam_ref):
    xcb = xc.astype(BF16)
    r = _sigmoid(jnp.dot(xcb, wa_ref[0].astype(BF16), preferred_element_type=F32) + ba_ref[...])
    i = _sigmoid(jnp.dot(xcb, wx_ref[0].astype(BF16), preferred_element_type=F32) + bx_ref[...])
    log_a = -LRU_C * r * _softplus(-lam_ref[...])
    a = jnp.exp(log_a)
    mult = jnp.sqrt(1.0 - a * a)
    return a, mult * (i * xc), _gelu_tanh(gate)


def _lru_prompt_body(xb_ref, gate_ref, cw_ref, cb_ref, wa_ref, wx_ref, ba_ref, bx_ref, lam_ref,
                     y_ref, tail_ref, h_ref, prev_ref, hc_ref):
    c = pl.program_id(2)

    @pl.when(c == 0)
    def _():
        prev_ref[...] = jnp.zeros_like(prev_ref)
        hc_ref[...] = jnp.zeros_like(hc_ref)

    xb = xb_ref[...]
    tt = xb.shape[0]
    prev8 = prev_ref[...]
    row8 = lax.broadcasted_iota(jnp.int32, prev8.shape, 0)
    cw = cw_ref[...]
    xc = cb_ref[...] + xb * cw[CONV_W - 1:CONV_W]
    for j in range(1, CONV_W):
        rolled = pltpu.roll(xb, j, 0)
        head = jnp.where(row8 < j, pltpu.roll(prev8, j, 0), rolled[:SUBLANES])
        xj = jnp.concatenate([head, rolled[SUBLANES:]], axis=0)
        xc = xc + xj * cw[CONV_W - 1 - j:CONV_W - j]

    a, b, gg = _lru_gates(xc, gate_ref[...], wa_ref, wx_ref, ba_ref, bx_ref, lam_ref)

    row = lax.broadcasted_iota(jnp.int32, a.shape, 0)
    s = 1
    while s < tt:
        keep = row >= s
        b = jnp.where(keep, a * pltpu.roll(b, s, 0) + b, b)
        a = jnp.where(keep, a * pltpu.roll(a, s, 0), a)
        s *= 2
    h = a * hc_ref[...] + b

    y_ref[...] = (h * gg).astype(y_ref.dtype)
    tail_ref[0] = xb[tt - SUBLANES:]
    h_ref[0] = h[tt - 1:]
    prev_ref[...] = xb[tt - SUBLANES:]
    hc_ref[...] = h[tt - 1:]


def lru_prompt(proj, p, j, bsz, t_len, tt=256):
    d = proj.shape[1] // 2
    nb, bw = p['lru_w_a'].shape[1], p['lru_w_a'].shape[2]
    tt = min(tt, t_len)
    assert t_len % tt == 0 and tt % SUBLANES == 0
    nc = t_len // tt
    vec = lambda a: a[j].reshape(1, d)
    vspec = pl.BlockSpec((1, bw), lambda b, n, c: (0, n))
    wspec = pl.BlockSpec((1, bw, bw), lambda b, n, c: (n, 0, 0))
    return pl.pallas_call(
        _lru_prompt_body,
        grid=(bsz, nb, nc),
        in_specs=[
            pl.BlockSpec((tt, bw), lambda b, n, c: (b * nc + c, n)),
            pl.BlockSpec((tt, bw), lambda b, n, c: (b * nc + c, nb + n)),
            pl.BlockSpec((CONV_W, bw), lambda b, n, c: (0, n)),
            vspec, wspec, wspec, vspec, vspec, vspec,
        ],
        out_specs=[
            pl.BlockSpec((tt, bw), lambda b, n, c: (b * nc + c, n)),
            pl.BlockSpec((1, SUBLANES, bw), lambda b, n, c: (b, 0, n)),
            pl.BlockSpec((1, 1, bw), lambda b, n, c: (b, 0, n)),
        ],
        out_shape=[
            jax.ShapeDtypeStruct((bsz * t_len, d), BF16),
            jax.ShapeDtypeStruct((bsz, SUBLANES, d), F32),
            jax.ShapeDtypeStruct((bsz, 1, d), F32),
        ],
        scratch_shapes=[pltpu.VMEM((SUBLANES, bw), F32), pltpu.VMEM((1, bw), F32)],
        compiler_params=_params("arbitrary", "arbitrary", "arbitrary"),
    )(proj, proj, p['lru_conv_w'][j], vec(p['lru_conv_b']), p['lru_w_a'][j], p['lru_w_x'][j],
      vec(p['lru_b_a']), vec(p['lru_b_x']), vec(p['lru_lam']))


def _lru_sample_body(xb_ref, gate_ref, cst_ref, h0_ref, cw_ref, cb_ref, wa_ref, wx_ref, ba_ref, bx_ref,
                     lam_ref, y_ref, tail_ref, h_ref, *, t_len, nseq):
    cw = cw_ref[...]
    xs = [cst_ref[q] for q in range(CONV_W - 1)]
    xs += [xb_ref[t * nseq:(t + 1) * nseq] for t in range(t_len)]
    xcs = []
    for t in range(t_len):
        xc = cb_ref[...] + xs[t] * cw[0:1]
        for q in range(1, CONV_W):
            xc = xc + xs[t + q] * cw[q:q + 1]
        xcs.append(xc)
    a, b, gg = _lru_gates(jnp.concatenate(xcs, axis=0), gate_ref[...], wa_ref, wx_ref, ba_ref, bx_ref, lam_ref)
    h = h0_ref[...]
    for t in range(t_len):
        sl = slice(t * nseq, (t + 1) * nseq)
        h = a[sl] * h + b[sl]
        y_ref[sl, :] = (h * gg[sl]).astype(y_ref.dtype)
    h_ref[...] = h
    for q in range(CONV_W - 1):
        tail_ref[q] = xs[t_len + q]


def lru_sample(proj, conv_st, h0, p, j, mp, nseq, t_len):
    d = proj.shape[1] // 2
    nb, bw = p['lru_w_a'].shape[1], p['lru_w_a'].shape[2]
    ms = nseq * t_len
    assert mp % ms == 0
    blk0 = mp // ms
    vec = lambda a: a[j].reshape(1, d)
    vspec = pl.BlockSpec((1, bw), lambda n: (0, n))
    wspec = pl.BlockSpec((1, bw, bw), lambda n: (n, 0, 0))
    return pl.pallas_call(
        functools.partial(_lru_sample_body, t_len=t_len, nseq=nseq),
        grid=(nb,),
        in_specs=[
            pl.BlockSpec((ms, bw), lambda n: (blk0, n)),
            pl.BlockSpec((ms, bw), lambda n: (blk0, nb + n)),
            pl.BlockSpec((CONV_W - 1, nseq, bw), lambda n: (0, 0, n)),
            pl.BlockSpec((nseq, bw), lambda n: (0, n)),
            pl.BlockSpec((CONV_W, bw), lambda n: (0, n)),
            vspec, wspec, wspec, vspec, vspec, vspec,
        ],
        out_specs=[
            pl.BlockSpec((ms, bw), lambda n: (0, n)),
            pl.BlockSpec((CONV_W - 1, nseq, bw), lambda n: (0, 0, n)),
            pl.BlockSpec((nseq, bw), lambda n: (0, n)),
        ],
        out_shape=[
            jax.ShapeDtypeStruct((ms, d), BF16),
            jax.ShapeDtypeStruct((CONV_W - 1, nseq, d), F32),
            jax.ShapeDtypeStruct((nseq, d), F32),
        ],
        compiler_params=_params("arbitrary"),
    )(proj, proj, conv_st, h0, p['lru_conv_w'][j], vec(p['lru_conv_b']), p['lru_w_a'][j], p['lru_w_x'][j],
      vec(p['lru_b_a']), vec(p['lru_b_x']), vec(p['lru_lam']))


def _mix_store(x, prev, mu_ref, out_refs):
    xx = prev - x
    for q, o_ref in enumerate(out_refs):
        o_ref[...] = (x + xx * mu_ref[q:q + 1]).astype(o_ref.dtype)


def _shift_prompt_body(x_ref, mu_ref, *rest, blocks_per_seq):
    out_refs, carry_ref = rest[:-1], rest[-1]
    i = pl.program_id(1)
    x = x_ref[...]
    first = jnp.where(i % blocks_per_seq == 0, jnp.zeros_like(carry_ref[...]), carry_ref[...])
    row = lax.broadcasted_iota(jnp.int32, x.shape, 0)
    prev = jnp.where(row == 0, first, pltpu.roll(x, 1, 0))
    _mix_store(x, prev, mu_ref, out_refs)
    carry_ref[...] = x[x.shape[0] - 1:]


def shift_prompt(x, mu, mp, t_len, tm=256, tc=1024):
    d = x.shape[1]
    tm, tc = min(tm, t_len), min(tc, d)
    assert t_len % tm == 0 and d % tc == 0
    nq = mu.shape[0]
    spec = pl.BlockSpec((tm, tc), lambda c, i: (i, c))
    return pl.pallas_call(
        functools.partial(_shift_prompt_body, blocks_per_seq=t_len // tm),
        grid=(d // tc, mp // tm),
        in_specs=[spec, pl.BlockSpec((nq, tc), lambda c, i: (0, c))],
        out_specs=[spec] * nq,
        out_shape=[jax.ShapeDtypeStruct((mp, d), BF16)] * nq,
        scratch_shapes=[pltpu.VMEM((1, tc), F32)],
        compiler_params=_params("arbitrary", "arbitrary"),
    )(x, mu)


def _shift_sample_body(x_ref, st_ref, mu_ref, *out_refs, nseq):
    x = x_ref[...]
    prev = jnp.concatenate([st_ref[...], x[:x.shape[0] - nseq]], axis=0)
    _mix_store(x, prev, mu_ref, out_refs)


def shift_sample(x, state, mu, mp, nseq, t_len, tc=1024):
    d = x.shape[1]
    tc = min(tc, d)
    ms = nseq * t_len
    blk0 = mp // ms
    nq = mu.shape[0]
    return pl.pallas_call(
        functools.partial(_shift_sample_body, nseq=nseq),
        grid=(d // tc,),
        in_specs=[pl.BlockSpec((ms, tc), lambda c: (blk0, c)),
                  pl.BlockSpec((nseq, tc), lambda c: (0, c)),
                  pl.BlockSpec((nq, tc), lambda c: (0, c))],
        out_specs=[pl.BlockSpec((ms, tc), lambda c: (0, c))] * nq,
        out_shape=[jax.ShapeDtypeStruct((ms, d), BF16)] * nq,
        compiler_params=_params("arbitrary"),
    )(x, state, mu)


def _seg_sum(x):
    half_sum = ((lax.broadcasted_iota(jnp.int32, (2 * PAIR, PAIR), 0) % PAIR) // RWKV_HEAD
                == lax.broadcasted_iota(jnp.int32, (2 * PAIR, PAIR), 1) // RWKV_HEAD).astype(BF16)
    hi = x.astype(BF16)
    lo = (x - hi.astype(F32)).astype(BF16)
    return jnp.dot(jnp.concatenate([hi, lo], axis=1), half_sum, preferred_element_type=F32)


def _wkv_parse(refs, has_vres, has_state, n_scratch):
    it = iter(refs)
    acts = tuple(next(it) for _ in range(6))
    vres = tuple(next(it) for _ in range(3)) if has_vres else None
    vecs = tuple(next(it) for _ in range(7))
    s0_ref = next(it) if has_state else None
    y_ref, s_ref = next(it), next(it)
    scratch = tuple(next(it) for _ in range(n_scratch))
    return acts, vres, vecs, s0_ref, y_ref, s_ref, scratch


def _wkv_operands(acts, vres, vecs, n_pairs):
    r_ref, k_ref, v_ref, wr_ref, ar_ref, _ = acts
    w0_ref, a0_ref, kk_ref, ka_ref, rk_ref, _, _ = vecs
    r = r_ref[...]
    k = k_ref[...]
    v = v_ref[...]
    w = -_softplus(-(w0_ref[...] + wr_ref[...])) - 0.5
    a = _sigmoid(a0_ref[...] + ar_ref[...])
    if vres is not None:
        vr_ref, vf_ref, v0_ref = vres
        v = v + (vf_ref[...] - v) * _sigmoid(v0_ref[...] + vr_ref[...])
    k2 = k * (1.0 + (a - 1.0) * ka_ref[...])
    kks = []
    for pr in range(n_pairs):
        ls = slice(pr * PAIR, (pr + 1) * PAIR)
        kkr = k[:, ls] * kk_ref[:, ls]
        kks.append(kkr / jnp.maximum(jnp.sqrt(_seg_sum(kkr * kkr)), 1e-12))
    kk = kks[0] if n_pairs == 1 else jnp.concatenate(kks, axis=1)
    return r, v, k2, kk, kk * a, -jnp.exp(w), r * k2 * rk_ref[...]


def _wkv_finish(ys_s, bonus_terms, v, acts, vecs, y_ref, n_pairs):
    g_ref, lg_ref, lb_ref = acts[5], vecs[5], vecs[6]
    for pr in range(n_pairs):
        ls = slice(pr * PAIR, (pr + 1) * PAIR)
        y = ys_s[:, ls]
        mu = _seg_sum(y) * (1.0 / RWKV_HEAD)
        yc = y - mu
        var = _seg_sum(yc * yc) * (1.0 / RWKV_HEAD)
        yn = yc * lax.rsqrt(var + GN_EPS) * lg_ref[:, ls] + lb_ref[:, ls]
        bonus = _seg_sum(bonus_terms[:, ls]) * v[:, ls]
        y_ref[:, ls] = ((yn + bonus) * g_ref[:, ls]).astype(y_ref.dtype)


def _wkv_steps_body(*refs, nb, nt, n_pairs, has_vres):
    acts, vres, vecs, s0_ref, y_ref, s_ref, scratch = _wkv_parse(refs, has_vres, True, 6)
    dec_s, kkn_s, kab_s, k2_s, v_s, ys_s = scratch
    r_ref = acts[0]
    s_ref[...] = s0_ref[...]
    _, v, k2, kk, kka, logdec, bonus_terms = _wkv_operands(acts, vres, vecs, n_pairs)
    dec_s[...] = jnp.exp(logdec)
    kkn_s[...] = -kk
    kab_s[...] = kka
    k2_s[...] = k2
    v_s[...] = v

    eye = (lax.broadcasted_iota(jnp.int32, (RWKV_HEAD, PAIR), 1) % RWKV_HEAD
           == lax.broadcasted_iota(jnp.int32, (RWKV_HEAD, PAIR), 0))
    sub8 = lax.broadcasted_iota(jnp.int32, (SUBLANES, PAIR), 0)
    operand_refs = (kkn_s, v_s, dec_s, kab_s, k2_s, r_ref)

    def seg_sums(xs):
        out = _seg_sum(jnp.concatenate(xs, axis=0))
        return [out[q * RWKV_HEAD:(q + 1) * RWKV_HEAD] for q in range(len(xs))]

    chains = range(SUBLANES)
    for t in range(nt):
        def seq_group(gi, carry):
            b0 = gi * SUBLANES
            rows = pl.ds(pl.multiple_of(t * nb + b0, SUBLANES), SUBLANES)
            for pr in range(n_pairs):
                ls = slice(pr * PAIR, (pr + 1) * PAIR)
                kkn, vv, dec, kab, k2r, rr = (ref[rows, ls] for ref in operand_refs)
                s = [s_ref[b0 + u, pr] for u in chains]
                sums = seg_sums([s[u] * kkn[u:u + 1] for u in chains]
                                + [jnp.where(eye, vv[u:u + 1], 0.0) for u in chains])
                s = [s[u] * dec[u:u + 1] + sums[u] * kab[u:u + 1] + sums[SUBLANES + u] * k2r[u:u + 1]
                     for u in chains]
                ycol = seg_sums([s[u] * rr[u:u + 1] for u in chains])
                ytile = jnp.zeros((SUBLANES, PAIR), F32)
                for u in chains:
                    s_ref[b0 + u, pr] = s[u]
                    yrow = jnp.sum(jnp.where(eye, ycol[u], 0.0), axis=0, keepdims=True)
                    ytile = jnp.where(sub8 == u, yrow, ytile)
                ys_s[rows, ls] = ytile
            return carry

        lax.fori_loop(0, nb // SUBLANES, seq_group, 0)

    _wkv_finish(ys_s, bonus_terms, v, acts, vecs, y_ref, n_pairs)


def _bdot(a, b, dims=(((1,), (0,)), ((), ()))):
    return lax.dot_general(a.astype(BF16), b.astype(BF16), dims, preferred_element_type=F32)


_NT = (((1,), (1,)), ((), ()))
_TN = (((0,), (0,)), ((), ()))


def _wkv_chunk_body(*refs, n_pairs, has_vres, chunk):
    acts, vres, vecs, _, y_ref, z_ref, scratch = _wkv_parse(refs, has_vres, False, 6)
    ld_s, kk_s, ka_s, k2_s, v_s, ys_s = scratch
    r_ref = acts[0]

    @pl.when(pl.program_id(2) == 0)
    def _():
        z_ref[...] = jnp.zeros_like(z_ref)

    _, v, k2, kk, kka, logdec, bonus_terms = _wkv_operands(acts, vres, vecs, n_pairs)
    ld_s[...] = logdec
    kk_s[...] = kk
    ka_s[...] = kka
    k2_s[...] = k2
    v_s[...] = v

    row_c = lax.broadcasted_iota(jnp.int32, (chunk, PAIR), 0)
    lo_c = lax.broadcasted_iota(jnp.int32, (chunk, PAIR), 1) < RWKV_HEAD
    i2 = lax.broadcasted_iota(jnp.int32, (2 * chunk, 2 * chunk), 0)
    j2 = lax.broadcasted_iota(jnp.int32, (2 * chunk, 2 * chunk), 1)
    same = (i2 // chunk) == (j2 // chunk)
    strict = same & ((i2 % chunk) > (j2 % chunk))
    incl = same & ((i2 % chunk) >= (j2 % chunk))
    ik = lax.broadcasted_iota(jnp.int32, (PAIR, PAIR), 0)
    jk = lax.broadcasted_iota(jnp.int32, (PAIR, PAIR), 1)
    n2 = 2 * chunk

    def stack2(x):
        return jnp.concatenate([jnp.where(lo_c, x, 0.0), jnp.where(lo_c, 0.0, x)], axis=0)

    def chunk_step(ci, carry):
        rows = pl.ds(pl.multiple_of(ci * chunk, chunk), chunk)
        pairs = range(n_pairs)
        lss = [slice(pr * PAIR, (pr + 1) * PAIR) for pr in pairs]
        qk, qr, kcat, kag, kkg, vb, g_col = ([] for _ in range(7))
        for ls in lss:
            ld = ld_s[rows, ls]
            cum = ld
            s = 1
            while s < chunk:
                cum = cum + jnp.where(row_c >= s, pltpu.roll(cum, s, 0), 0.0)
                s *= 2
            lwc = cum[chunk - 1:chunk]
            kk_c, ka_c, k2_c = kk_s[rows, ls], ka_s[rows, ls], k2_s[rows, ls]
            e_inv = jnp.exp(-cum)
            e_end = jnp.exp(lwc - cum)
            qk.append(stack2(kk_c * jnp.exp(cum - ld)).astype(BF16))
            qr.append(stack2(r_ref[rows, ls] * jnp.exp(cum)))
            kcat.append(jnp.concatenate([stack2(ka_c * e_inv), stack2(k2_c * e_inv)], axis=0).astype(BF16))
            kag.append(stack2(ka_c * e_end).astype(BF16))
            kkg.append(stack2(k2_c * e_end).astype(BF16))
            vb.append(stack2(v_s[rows, ls]).astype(BF16))
            g_col.append(jnp.sum(jnp.where(ik == jk, jnp.exp(lwc), 0.0), axis=1, keepdims=True))
        sc = [_bdot(jnp.concatenate([qk[p], qr[p].astype(BF16)], axis=0), kcat[p], _NT) for p in pairs]
        a_ak = [jnp.where(strict, sc[p][:n2, :n2], 0.0) for p in pairs]
        a_kk = [jnp.where(strict, sc[p][:n2, n2:], 0.0).astype(BF16) for p in pairs]
        b_ak = [jnp.where(incl, sc[p][n2:, :n2], 0.0).astype(BF16) for p in pairs]
        b_kk = [jnp.where(incl, sc[p][n2:, n2:], 0.0).astype(BF16) for p in pairs]
        tinv = [jnp.where(i2 == j2, 1.0, 0.0) - a for a in a_ak]
        apow = [a.astype(BF16) for a in a_ak]
        s = 1
        while 2 * s < chunk:
            apow = [_bdot(a, a).astype(BF16) for a in apow]
            tinv = [t + _bdot(t, a) for t, a in zip(tinv, apow)]
            s *= 2
        w = [_bdot(a_kk[p], vb[p]).astype(BF16) for p in pairs]
        pu = [(-_bdot(tinv[p], jnp.concatenate([qk[p], w[p]], axis=1))).astype(BF16) for p in pairs]
        bp = [_bdot(b_ak[p], pu[p]) for p in pairs]
        kp = [_bdot(kag[p], pu[p], _TN) for p in pairs]
        y0 = [bp[p][:, PAIR:] + _bdot(b_kk[p], vb[p]) for p in pairs]
        z0 = [kp[p][:, PAIR:] + _bdot(kkg[p], vb[p], _TN) for p in pairs]
        for p in pairs:
            z = z_ref[0, p]
            zb = z.astype(BF16)
            yb = _bdot(qr[p] + bp[p][:, :PAIR], zb) + y0[p]
            ys_s[rows, lss[p]] = yb[:chunk] + yb[chunk:]
            z_ref[0, p] = g_col[p] * z + _bdot(kp[p][:, :PAIR], zb) + z0[p]
        return carry

    lax.fori_loop(0, r_ref.shape[0] // chunk, chunk_step, 0)
    _wkv_finish(ys_s, bonus_terms, v, acts, vecs, y_ref, n_pairs)


def _wkv_call(body, acts, vres, p, j, state, *, grid, rows, n_pairs, act_map, out_map, vec_map, st_map,
              out_rows, state_shape, state_block):
    d = acts[0].shape[1]
    lw = n_pairs * PAIR
    aspec = pl.BlockSpec((rows, lw), act_map)
    vspec = pl.BlockSpec((1, lw), vec_map)
    sspec = pl.BlockSpec(state_block, st_map)
    vec = lambda a: a.reshape(1, d)
    args, in_specs = list(acts), [aspec] * 6
    if vres is not None:
        v_raw, v_first, v0 = vres
        args += [v_raw, v_first, vec(v0)]
        in_specs += [aspec, aspec, vspec]
    args += [vec(p['rw_w0'][j]), vec(p['rw_a0'][j]), vec(p['rw_k_k'][j]), vec(p['rw_k_a'][j]),
             vec(p['rw_r_k'][j]), vec(p['rw_lnx_g'][j]), vec(p['rw_lnx_b'][j])]
    in_specs += [vspec] * 7
    if state is not None:
        args.append(state)
        in_specs.append(sspec)
    return pl.pallas_call(
        body,
        grid=grid,
        in_specs=in_specs,
        out_specs=[pl.BlockSpec((rows, lw), out_map), sspec],
        out_shape=[jax.ShapeDtypeStruct((out_rows, d), BF16), jax.ShapeDtypeStruct(state_shape, F32)],
        scratch_shapes=[pltpu.VMEM((rows, lw), F32)] * 6,
        compiler_params=_params(*(["arbitrary"] * len(grid))),
    )(*args)


def wkv_prompt(acts, vres, p, j, bsz, t_len, n_pairs=8, rows=256, chunk=64):
    d = acts[0].shape[1]
    n_pairs = min(n_pairs, d // PAIR)
    rows = min(rows, t_len)
    assert t_len % rows == 0 and rows % chunk == 0
    nc = t_len // rows
    return _wkv_call(
        functools.partial(_wkv_chunk_body, n_pairs=n_pairs, has_vres=vres is not None, chunk=chunk),
        acts, vres, p, j, None, grid=(bsz, d // (n_pairs * PAIR), nc), rows=rows, n_pairs=n_pairs,
        act_map=lambda b, l, c: (b * nc + c, l), out_map=lambda b, l, c: (b * nc + c, l),
        vec_map=lambda b, l, c: (0, l), st_map=lambda b, l, c: (b, l, 0, 0),
        out_rows=bsz * t_len, state_shape=(bsz, d // PAIR, PAIR, PAIR), state_block=(1, n_pairs, PAIR, PAIR))


def wkv_sample(acts, vres, p, j, state, row_blk0, nseq, t_len):
    d = acts[0].shape[1]
    return _wkv_call(
        functools.partial(_wkv_steps_body, nb=nseq, nt=t_len, n_pairs=1, has_vres=vres is not None),
        acts, vres, p, j, state, grid=(d // PAIR,), rows=nseq * t_len, n_pairs=1,
        act_map=lambda l: (row_blk0, l), out_map=lambda l: (0, l), vec_map=lambda l: (0, l),
        st_map=lambda l: (0, l, 0, 0), out_rows=nseq * t_len, state_shape=(nseq, d // PAIR, RWKV_HEAD, PAIR),
        state_block=(nseq, 1, RWKV_HEAD, PAIR))


def _pair_state(s):
    b, h, nv, nk = s.shape
    return s.reshape(b, h // 2, 2, nv, nk).transpose(0, 1, 3, 2, 4).reshape(b, h // 2, nv, 2 * nk)


def _unpair_state(s):
    b, hp, nv, nk2 = s.shape
    return s.reshape(b, hp, nv, 2, nk2 // 2).transpose(0, 1, 3, 2, 4).reshape(b, hp * 2, nv, nk2 // 2)


def _state_from_blockdiag(z):
    b, hp = z.shape[:2]
    z = z.reshape(b, hp, 2, RWKV_HEAD, 2, RWKV_HEAD)
    diag = jnp.stack([z[:, :, 0, :, 0, :], z[:, :, 1, :, 1, :]], axis=2)
    return diag.transpose(0, 1, 2, 4, 3).reshape(b, hp * 2, RWKV_HEAD, RWKV_HEAD)


def _route(logits, n_groups, n_experts, tm):
    m = logits.shape[0]
    epg = n_experts // n_groups
    g_logits = logits[:, :n_groups]
    g_idx = jnp.argmax(g_logits, axis=-1)
    p_group = jnp.max(jax.nn.softmax(g_logits, axis=-1), axis=-1, keepdims=True)
    e_logits = logits[:, n_groups:n_groups + n_experts].reshape(m, n_groups, epg)
    e_in = jnp.take_along_axis(e_logits, g_idx[:, None, None], axis=1)[:, 0]
    top_val, top_idx = lax.top_k(e_in, TOP_K)
    gate = p_group * jax.nn.softmax(top_val, axis=-1)
    expert_flat = (g_idx[:, None] * epg + top_idx).astype(jnp.int32).reshape(-1)

    n_assign = m * TOP_K
    order = jnp.argsort(expert_flat).astype(jnp.int32)
    e_sorted = expert_flat[order]
    tok_sorted = order // TOP_K
    counts = jnp.sum(expert_flat[:, None] == jnp.arange(n_experts, dtype=jnp.int32)[None, :], axis=0, dtype=jnp.int32)
    start = jnp.cumsum(counts) - counts
    nblk_e = (counts + tm - 1) // tm
    blk_end = jnp.cumsum(nblk_e)
    blk_start = blk_end - nblk_e
    slot_sorted = blk_start[e_sorted] * tm + (jnp.arange(n_assign, dtype=jnp.int32) - start[e_sorted])
    slot_of_assign = slot_sorted[jnp.argsort(order)]
    n_blocks = (n_assign + n_experts * (tm - 1) + tm - 1) // tm
    blk = jnp.arange(n_blocks, dtype=jnp.int32)
    blk_expert = jnp.minimum(jnp.sum(blk[:, None] >= blk_end[None, :], axis=1), n_experts - 1)
    e_slot = jnp.repeat(blk_expert, tm)
    off = jnp.arange(n_blocks * tm, dtype=jnp.int32) - blk_start[e_slot] * tm
    valid = (off >= 0) & (off < counts[e_slot])
    slot_tok = jnp.where(valid, tok_sorted[jnp.clip(start[e_slot] + off, 0, n_assign - 1)], 0)
    return slot_tok, slot_of_assign.reshape(m, TOP_K), gate.astype(F32), blk_start, nblk_e


def _gather_body(tok_ref, x_hbm, o_ref, buf_ref, sem, *, tm):
    i = pl.program_id(0)

    def copy(step, r):
        slot = step % 2
        return pltpu.make_async_copy(x_hbm.at[tok_ref[step * tm + r]], buf_ref.at[slot, r], sem.at[slot])

    def issue(step):
        lax.fori_loop(0, tm, lambda r, c: (copy(step, r).start(), c)[1], 0)

    @pl.when(i == 0)
    def _():
        issue(0)

    @pl.when(i + 1 < pl.num_programs(0))
    def _():
        issue(i + 1)

    lax.fori_loop(0, tm, lambda r, c: (copy(i, r).wait(), c)[1], 0)
    for c in range(buf_ref.shape[2]):
        o_ref[:, c * LANES:(c + 1) * LANES] = buf_ref[i % 2, :, c, :].astype(o_ref.dtype)


def _token_major(x):
    return x.reshape(x.shape[0], x.shape[1] // LANES, LANES)


def gather_rows(x, slot_tok, tm=256):
    d = x.shape[1] * x.shape[2]
    n_slots = slot_tok.shape[0]
    tm = _div_tile(n_slots, tm)
    return pl.pallas_call(
        functools.partial(_gather_body, tm=tm),
        grid_spec=pltpu.PrefetchScalarGridSpec(
            num_scalar_prefetch=1,
            grid=(n_slots // tm,),
            in_specs=[pl.BlockSpec(memory_space=pl.ANY)],
            out_specs=pl.BlockSpec((tm, d), lambda i, tok: (i, 0)),
            scratch_shapes=[pltpu.VMEM((2, tm) + x.shape[1:], F32), pltpu.SemaphoreType.DMA((2,))],
        ),
        out_shape=jax.ShapeDtypeStruct((n_slots, d), BF16),
        compiler_params=_params("arbitrary"),
    )(slot_tok, x)


def _store_block(ref, value):
    ref[...] = value.astype(ref.dtype)


def _store_block_token_major(ref, value):
    for c in range(ref.shape[1]):
        ref[:, c, :] = value[:, c * LANES:(c + 1) * LANES].astype(ref.dtype)


def _expert_blocks(bs_ref, nb_ref, src_hbm, dst_hbm, xbuf, obuf, xsem, osem, compute, *, tm, dst_cols,
                   store=_store_block):
    e = pl.program_id(0)
    row0 = bs_ref[e] * tm
    n = nb_ref[e]

    def rows(b):
        return pl.ds(pl.multiple_of(row0 + b * tm, tm), tm)

    def xcopy(b):
        return pltpu.make_async_copy(src_hbm.at[rows(b)], xbuf.at[b % 2], xsem.at[b % 2])

    def ocopy(b):
        dst = dst_hbm.at[rows(b)] if dst_cols is None else dst_hbm.at[rows(b), dst_cols]
        return pltpu.make_async_copy(obuf.at[b % 2], dst, osem.at[b % 2])

    @pl.when(n > 0)
    def _():
        xcopy(0).start(priority=BLOCK_DMA_PRIORITY)

    def block(b, c):
        xcopy(b).wait()

        @pl.when(b + 1 < n)
        def _():
            xcopy(b + 1).start(priority=BLOCK_DMA_PRIORITY)

        @pl.when(b >= 2)
        def _():
            ocopy(b - 2).wait()

        store(obuf.at[b % 2], compute(xbuf[b % 2]))
        ocopy(b).start(priority=BLOCK_DMA_PRIORITY)
        return c

    lax.fori_loop(0, n, block, 0)

    @pl.when(n >= 2)
    def _():
        ocopy(n - 2).wait()

    @pl.when(n >= 1)
    def _():
        ocopy(n - 1).wait()

    @pl.when(e == pl.num_programs(0) - 1)
    def _():
        obuf[0] = jnp.zeros(obuf.shape[1:], obuf.dtype)

        def fill(blk, c):
            fill_rows = pl.ds(pl.multiple_of(blk * tm, tm), tm)
            dst = dst_hbm.at[fill_rows] if dst_cols is None else dst_hbm.at[fill_rows, dst_cols]
            cp = pltpu.make_async_copy(obuf.at[0], dst, osem.at[0])
            cp.start()
            cp.wait()
            return c

        lax.fori_loop(bs_ref[e] + n, dst_hbm.shape[0] // tm, fill, 0)


def _ffn_up_body(bs_ref, nb_ref, x_hbm, wg_ref, wu_ref, o_hbm, wgbf_ref, wubf_ref, xbuf, obuf, xsem, osem, *, tm, tn):
    wgbf_ref[...] = wg_ref[...].astype(BF16)
    wubf_ref[...] = wu_ref[...].astype(BF16)

    def compute(x):
        hg = jnp.dot(x, wgbf_ref[...], preferred_element_type=F32)
        hu = jnp.dot(x, wubf_ref[...], preferred_element_type=F32)
        return hg * _sigmoid(hg) * hu

    cols = pl.ds(pl.multiple_of(pl.program_id(1) * tn, tn), tn)
    _expert_blocks(bs_ref, nb_ref, x_hbm, o_hbm, xbuf, obuf, xsem, osem, compute, tm=tm, dst_cols=cols)


def _ffn_down_body(bs_ref, nb_ref, h_hbm, wd_ref, o_hbm, wdbf_ref, xbuf, obuf, xsem, osem, *, tm):
    wdbf_ref[...] = wd_ref[...].astype(BF16)

    def compute(h):
        return jnp.dot(h, wdbf_ref[...], preferred_element_type=F32)

    _expert_blocks(bs_ref, nb_ref, h_hbm, o_hbm, xbuf, obuf, xsem, osem, compute, tm=tm, dst_cols=None,
                   store=_store_block_token_major)


def expert_ffn(x_sorted, blk_start, nblk_e, w_gate, w_up, w_down, layer, tm, tn_up=512):
    n_slots, d = x_sorted.shape
    n_experts, de = w_gate.shape[1], w_gate.shape[3]
    tn_up = min(tn_up, de)
    any_spec = pl.BlockSpec(memory_space=pl.ANY)
    dma2 = pltpu.SemaphoreType.DMA((2,))
    hid = pl.pallas_call(
        functools.partial(_ffn_up_body, tm=tm, tn=tn_up),
        grid_spec=pltpu.PrefetchScalarGridSpec(
            num_scalar_prefetch=2,
            grid=(n_experts, de // tn_up),
            in_specs=[any_spec,
                      pl.BlockSpec((None, None, d, tn_up), lambda e, j, bs, nb: (layer, e, 0, j)),
                      pl.BlockSpec((None, None, d, tn_up), lambda e, j, bs, nb: (layer, e, 0, j))],
            out_specs=any_spec,
            scratch_shapes=[pltpu.VMEM((d, tn_up), BF16), pltpu.VMEM((d, tn_up), BF16),
                            pltpu.VMEM((2, tm, d), BF16), pltpu.VMEM((2, tm, tn_up), BF16), dma2, dma2],
        ),
        out_shape=jax.ShapeDtypeStruct((n_slots, de), BF16),
        compiler_params=_params("arbitrary", "arbitrary"),
    )(blk_start, nblk_e, x_sorted, w_gate, w_up)
    return pl.pallas_call(
        functools.partial(_ffn_down_body, tm=tm),
        grid_spec=pltpu.PrefetchScalarGridSpec(
            num_scalar_prefetch=2,
            grid=(n_experts,),
            in_specs=[any_spec, pl.BlockSpec((None, None, de, d), lambda e, bs, nb: (layer, e, 0, 0))],
            out_specs=any_spec,
            scratch_shapes=[pltpu.VMEM((de, d), BF16), pltpu.VMEM((2, tm, de), BF16),
                            pltpu.VMEM((2, tm, d // LANES, LANES), F32), dma2, dma2],
        ),
        out_shape=jax.ShapeDtypeStruct((n_slots, d // LANES, LANES), F32),
        compiler_params=_params("arbitrary"),
    )(blk_start, nblk_e, hid, w_down)


def _combine_ln_body(slot_ref, x_ref, gate_ref, g_ref, b_ref, y_hbm, o_ref, obf_ref, buf_ref, sem, *, tm, alpha):
    i = pl.program_id(0)

    def copy(step, q):
        slot = step % 2
        return pltpu.make_async_copy(y_hbm.at[slot_ref[step * tm * TOP_K + q]],
                                     buf_ref.at[slot, q % TOP_K, q // TOP_K], sem.at[slot])

    def issue(step):
        lax.fori_loop(0, tm * TOP_K, lambda q, c: (copy(step, q).start(), c)[1], 0)

    @pl.when(i == 0)
    def _():
        issue(0)

    @pl.when(i + 1 < pl.num_programs(0))
    def _():
        issue(i + 1)

    lax.fori_loop(0, tm * TOP_K, lambda q, c: (copy(i, q).wait(), c)[1], 0)
    gate = gate_ref[...]
    buf = buf_ref.at[i % 2]
    pieces = []
    for c in range(buf_ref.shape[3]):
        piece = buf[0, :, c, :] * gate[:, 0:1]
        for q in range(1, TOP_K):
            piece = piece + buf[q, :, c, :] * gate[:, q:q + 1]
        pieces.append(piece)
    ffn = jnp.concatenate(pieces, axis=1)
    y = _layer_norm_rows(alpha * x_ref[...] + ffn, g_ref[...], b_ref[...])
    o_ref[...] = y
    obf_ref[...] = y.astype(BF16)


def combine_layer_norm(x, out_slots, slot_of_assign, gate, g, b, alpha, tm=256):
    m, d = x.shape
    tm = _div_tile(m, tm)
    row = pl.BlockSpec((tm, d), lambda i, s: (i, 0))
    vec = pl.BlockSpec((1, d), lambda i, s: (0, 0))
    return pl.pallas_call(
        functools.partial(_combine_ln_body, tm=tm, alpha=alpha),
        grid_spec=pltpu.PrefetchScalarGridSpec(
            num_scalar_prefetch=1,
            grid=(m // tm,),
            in_specs=[row, pl.BlockSpec((tm, TOP_K), lambda i, s: (i, 0)), vec, vec,
                      pl.BlockSpec(memory_space=pl.ANY)],
            out_specs=[row, row],
            scratch_shapes=[pltpu.VMEM((2, TOP_K, tm, d // LANES, LANES), F32), pltpu.SemaphoreType.DMA((2,))],
        ),
        out_shape=[jax.ShapeDtypeStruct((m, d), F32), jax.ShapeDtypeStruct((m, d), BF16)],
        compiler_params=_params("arbitrary"),
    )(slot_of_assign.reshape(-1), x, gate, g.reshape(1, d), b.reshape(1, d), out_slots)


def _trunk(x_prompt, x_sample, state_conv, state_h, state_shift, state_wkv, p, moe_tm=128):
    bsz, t_len, d = x_prompt.shape
    nseq, dec_t, _ = x_sample.shape
    mp, ms = bsz * t_len, nseq * dec_t
    depth = p['ln_g'].shape[0]
    alpha = (2 * depth) ** 0.25
    n_groups = p['moe_w_group'].shape[2]
    n_experts = p['moe_w_expert'].shape[2]

    x = jnp.concatenate([x_prompt.reshape(mp, d), x_sample.transpose(1, 0, 2).reshape(ms, d)], axis=0)
    xbf = x.astype(BF16)
    conv_s_tm = state_conv.transpose(0, 2, 1, 3)

    conv_p, h_p, shift_p, wkv_p, conv_s, h_s, shift_s, wkv_s = ([] for _ in range(8))
    v_first = None
    for i in range(depth):
        j = i // 2
        if i % 2 == 0:
            proj = matmul(xbf, p['lru_w_in'], j, mp, ms)
            y_p, tail_p, hl_p = lru_prompt(proj, p, j, bsz, t_len)
            y_s, tail_s, hl_s = lru_sample(proj, conv_s_tm[j], state_h[j], p, j, mp, nseq, dec_t)
            mix = matmul((y_p, y_s), p['lru_w_out'], j, mp, ms)
            conv_p.append(tail_p[:, SUBLANES - (CONV_W - 1):])
            h_p.append(hl_p[:, 0])
            conv_s.append(tail_s.transpose(1, 0, 2))
            h_s.append(hl_s)
        else:
            mu = p['rw_mu'][j]
            mixed_p = shift_prompt(x, mu, mp, t_len)
            mixed_s = shift_sample(x, state_shift[j], mu, mp, nseq, dec_t)
            xr, xw, xk, xv, xa, xg = zip(mixed_p, mixed_s)
            r = matmul(xr, p['rw_w_r'], j, mp, ms)
            k = matmul(xk, p['rw_w_k'], j, mp, ms)
            v = matmul(xv, p['rw_w_v'], j, mp, ms)
            w_raw = lora(xw, p['rw_w1'][j], p['rw_w2'][j], mp, ms, "tanh")
            a_raw = lora(xa, p['rw_a1'][j], p['rw_a2'][j], mp, ms, "none")
            g = lora(xg, p['rw_g1'][j], p['rw_g2'][j], mp, ms, "sigmoid")
            if j == 0:
                vres, v_first = None, v
            else:
                v_raw = lora(xv, p['rw_v1'][j - 1], p['rw_v2'][j - 1], mp, ms, "none")
                vres = (v_raw, v_first, p['rw_v0'][j - 1])
            acts = (r, k, v, w_raw, a_raw, g)
            yg_p, z_p = wkv_prompt(acts, vres, p, j, bsz, t_len)
            yg_s, st_s = wkv_sample(acts, vres, p, j, _pair_state(state_wkv[j]), mp // ms, nseq, dec_t)
            mix = matmul((yg_p, yg_s), p['rw_w_o'], j, mp, ms)
            shift_p.append(x[t_len - 1:mp:t_len])
            shift_s.append(x[mp + (dec_t - 1) * nseq:])
            wkv_p.append(_state_from_blockdiag(z_p))
            wkv_s.append(_unpair_state(st_s))

        w_route = jnp.concatenate([p['moe_w_group'][i], p['moe_w_expert'][i]], axis=1)
        b_route = jnp.concatenate([p['moe_b_group'][i], p['moe_b_expert'][i]], axis=0)
        pad = ROUTE_PAD - w_route.shape[1]
        w_route = jnp.pad(w_route, ((0, 0), (0, pad)))
        b_route = jnp.pad(b_route, (0, pad)).reshape(1, ROUTE_PAD)
        x, xbf, logits = residual_layer_norm(x, mix, p['ln_g'][i, 0], p['ln_b'][i, 0], alpha,
                                             router=(w_route, b_route))
        slot_tok, slot_of_assign, gate, blk_start, nblk_e = _route(logits, n_groups, n_experts, moe_tm)
        x_sorted = gather_rows(_token_major(x), slot_tok)
        out_slots = expert_ffn(x_sorted, blk_start, nblk_e, p['moe_w_gate'], p['moe_w_up'], p['moe_w_down'],
                               i, moe_tm)
        x, xbf = combine_layer_norm(x, out_slots, slot_of_assign, gate, p['ln_g'][i, 1], p['ln_b'][i, 1], alpha)

    y_prompt = x[:mp].reshape(bsz, t_len, d)
    y_sample = x[mp:].reshape(dec_t, nseq, d).transpose(1, 0, 2)
    return (y_prompt, y_sample, jnp.stack(conv_p), jnp.stack(h_p), jnp.stack(shift_p), jnp.stack(wkv_p),
            jnp.stack(conv_s), jnp.stack(h_s), jnp.stack(shift_s), jnp.stack(wkv_s))


def kernel(x_prompt, x_sample, state_conv, state_h, state_shift, state_wkv, ln_g, ln_b, lru_w_in, lru_conv_w, lru_conv_b, lru_w_a, lru_b_a, lru_w_x, lru_b_x, lru_lam, lru_w_out, rw_mu, rw_w_r, rw_w_k, rw_w_v, rw_w_o, rw_w0, rw_w1, rw_w2, rw_a0, rw_a1, rw_a2, rw_v0, rw_v1, rw_v2, rw_g1, rw_g2, rw_k_k, rw_k_a, rw_r_k, rw_lnx_g, rw_lnx_b, moe_w_group, moe_b_group, moe_w_expert, moe_b_expert, moe_w_gate, moe_w_up, moe_w_down):
    p = dict(ln_g=ln_g, ln_b=ln_b, lru_w_in=lru_w_in, lru_conv_w=lru_conv_w, lru_conv_b=lru_conv_b,
             lru_w_a=lru_w_a, lru_b_a=lru_b_a, lru_w_x=lru_w_x, lru_b_x=lru_b_x, lru_lam=lru_lam,
             lru_w_out=lru_w_out, rw_mu=rw_mu, rw_w_r=rw_w_r, rw_w_k=rw_w_k, rw_w_v=rw_w_v, rw_w_o=rw_w_o,
             rw_w0=rw_w0, rw_w1=rw_w1, rw_w2=rw_w2, rw_a0=rw_a0, rw_a1=rw_a1, rw_a2=rw_a2,
             rw_v0=rw_v0, rw_v1=rw_v1, rw_v2=rw_v2, rw_g1=rw_g1, rw_g2=rw_g2, rw_k_k=rw_k_k, rw_k_a=rw_k_a,
             rw_r_k=rw_r_k, rw_lnx_g=rw_lnx_g, rw_lnx_b=rw_lnx_b, moe_w_group=moe_w_group,
             moe_b_group=moe_b_group, moe_w_expert=moe_w_expert, moe_b_expert=moe_b_expert,
             moe_w_gate=moe_w_gate, moe_w_up=moe_w_up, moe_w_down=moe_w_down)
    return _trunk(x_prompt, x_sample, state_conv, state_h, state_shift, state_wkv, p)
```

```python
import functools

import jax
import jax.numpy as jnp
from jax import lax
from jax.experimental import pallas as pl
from jax.experimental.pallas import tpu as pltpu

F32 = jnp.float32
BF16 = jnp.bfloat16

V7X_VMEM_LIMIT_BYTES = 56 * 1024 * 1024
LANES = 128
SUBLANES = 8
RWKV_HEAD = 64
PAIR = 2 * RWKV_HEAD
CONV_W = 4
LRU_C = 8.0
GN_EPS = 64e-5
LN_EPS = 1e-5
TOP_K = 2
ROUTE_PAD = LANES
N_WEIGHT_PIECES = 4


def _params(*sem):
    return pltpu.CompilerParams(dimension_semantics=sem, vmem_limit_bytes=V7X_VMEM_LIMIT_BYTES)


def _sigmoid(x):
    return 1.0 / (1.0 + jnp.exp(-x))


def _softplus(x):
    return jnp.maximum(x, 0.0) + jnp.log1p(jnp.exp(-jnp.abs(x)))


def _gelu_tanh(x):
    return 0.5 * x * (1.0 + jnp.tanh(0.7978845608028654 * (x + 0.044715 * (x * x * x))))


def _row_tile(mp, ms, cap):
    tm = min(cap, ms)
    assert mp % tm == 0 and ms % tm == 0 and tm % SUBLANES == 0, (mp, ms, tm)
    return tm


def _div_tile(m, cap):
    tm = min(cap, m) // SUBLANES * SUBLANES
    while m % tm:
        tm -= SUBLANES
    return tm


def _mm_body(xp_ref, xs_ref, w_ref, o_ref, wbf_ref, *, n_p):
    i = pl.program_id(1)

    @pl.when(i == 0)
    def _():
        wbf_ref[...] = w_ref[...].astype(BF16)

    def run(x_ref):
        o_ref[...] = jnp.dot(x_ref[...], wbf_ref[...], preferred_element_type=F32).astype(o_ref.dtype)

    @pl.when(i < n_p)
    def _():
        run(xp_ref)

    @pl.when(i >= n_p)
    def _():
        run(xs_ref)


def _split_rows(act, mp):
    if isinstance(act, tuple):
        return act[0], act[1], 0
    return act, act, mp


def matmul(act, w, layer, mp, ms, out_dtype=F32, tn=512, tm_cap=512):
    xp, xs, s_off = _split_rows(act, mp)
    _, k, n = w.shape
    tn = min(tn, n)
    assert n % tn == 0
    tm = _row_tile(mp, ms, tm_cap)
    n_p, n_s = mp // tm, ms // tm
    s_blk0 = s_off // tm
    return pl.pallas_call(
        functools.partial(_mm_body, n_p=n_p),
        grid=(n // tn, n_p + n_s),
        in_specs=[
            pl.BlockSpec((tm, k), lambda j, i: (jnp.minimum(i, n_p - 1), 0)),
            pl.BlockSpec((tm, k), lambda j, i: (s_blk0 + jnp.maximum(i - n_p, 0), 0)),
            pl.BlockSpec((None, k, tn), lambda j, i: (layer, 0, j)),
        ],
        out_specs=pl.BlockSpec((tm, tn), lambda j, i: (i, j)),
        out_shape=jax.ShapeDtypeStruct((mp + ms, n), out_dtype),
        scratch_shapes=[pltpu.VMEM((k, tn), BF16)],
        compiler_params=_params("arbitrary", "arbitrary"),
    )(xp, xs, w)


def _lora_body(xp_ref, xs_ref, w1_ref, w2_ref, o_ref, w1bf_ref, w2bf_ref, *, n_p, act):
    i = pl.program_id(0)

    @pl.when(i == 0)
    def _():
        w1bf_ref[...] = w1_ref[...].astype(BF16)
        w2bf_ref[...] = w2_ref[...].astype(BF16)

    def run(x_ref):
        t = jnp.dot(x_ref[...], w1bf_ref[...], preferred_element_type=F32)
        if act == "tanh":
            t = jnp.tanh(t)
        elif act == "sigmoid":
            t = _sigmoid(t)
        o_ref[...] = jnp.dot(t.astype(BF16), w2bf_ref[...], preferred_element_type=F32)

    @pl.when(i < n_p)
    def _():
        run(xp_ref)

    @pl.when(i >= n_p)
    def _():
        run(xs_ref)


def lora(act_in, w1, w2, mp, ms, act, tm_cap=256):
    xp, xs, s_off = _split_rows(act_in, mp)
    pad = -w1.shape[1] % LANES
    w1 = jnp.pad(w1, ((0, 0), (0, pad)))
    w2 = jnp.pad(w2, ((0, pad), (0, 0)))
    k, r = w1.shape
    n = w2.shape[1]
    tm = _row_tile(mp, ms, tm_cap)
    n_p, n_s = mp // tm, ms // tm
    s_blk0 = s_off // tm
    return pl.pallas_call(
        functools.partial(_lora_body, n_p=n_p, act=act),
        grid=(n_p + n_s,),
        in_specs=[
            pl.BlockSpec((tm, k), lambda i: (jnp.minimum(i, n_p - 1), 0)),
            pl.BlockSpec((tm, k), lambda i: (s_blk0 + jnp.maximum(i - n_p, 0), 0)),
            pl.BlockSpec((k, r), lambda i: (0, 0)),
            pl.BlockSpec((r, n), lambda i: (0, 0)),
        ],
        out_specs=pl.BlockSpec((tm, n), lambda i: (i, 0)),
        out_shape=jax.ShapeDtypeStruct((mp + ms, n), F32),
        scratch_shapes=[pltpu.VMEM((k, r), BF16), pltpu.VMEM((r, n), BF16)],
        compiler_params=_params("arbitrary"),
    )(xp, xs, w1, w2)


def _layer_norm_rows(z, g, b):
    mu = jnp.mean(z, axis=-1, keepdims=True)
    zc = z - mu
    var = jnp.mean(zc * zc, axis=-1, keepdims=True)
    return zc * lax.rsqrt(var + LN_EPS) * g + b


def _ln_body(x_ref, mix_ref, g_ref, b_ref, *rest, alpha, with_router):
    if with_router:
        wr_ref, br_ref, o_ref, obf_ref, lg_ref = rest
    else:
        o_ref, obf_ref = rest
    y = _layer_norm_rows(alpha * x_ref[...] + mix_ref[...], g_ref[...], b_ref[...])
    ybf = y.astype(BF16)
    o_ref[...] = y
    obf_ref[...] = ybf
    if with_router:
        lg_ref[...] = jnp.dot(ybf, wr_ref[...].astype(BF16), preferred_element_type=F32) + br_ref[...]


def residual_layer_norm(x, mix, g, b, alpha, router=None, tm=256):
    m, d = x.shape
    tm = _div_tile(m, tm)
    row = pl.BlockSpec((tm, d), lambda i: (i, 0))
    vec = pl.BlockSpec((1, d), lambda i: (0, 0))
    in_specs = [row, row, vec, vec]
    args = [x, mix, g.reshape(1, d), b.reshape(1, d)]
    out_specs = [row, row]
    out_shape = [jax.ShapeDtypeStruct((m, d), F32), jax.ShapeDtypeStruct((m, d), BF16)]
    if router is not None:
        wr, br = router
        in_specs += [pl.BlockSpec((d, ROUTE_PAD), lambda i: (0, 0)), pl.BlockSpec((1, ROUTE_PAD), lambda i: (0, 0))]
        args += [wr, br]
        out_specs.append(pl.BlockSpec((tm, ROUTE_PAD), lambda i: (i, 0)))
        out_shape.append(jax.ShapeDtypeStruct((m, ROUTE_PAD), F32))
    return pl.pallas_call(
        functools.partial(_ln_body, alpha=alpha, with_router=router is not None),
        grid=(m // tm,),
        in_specs=in_specs,
        out_specs=out_specs,
        out_shape=out_shape,
        compiler_params=_params("arbitrary"),
    )(*args)


def _lru_gates(xc, gate, wa_ref, wx_ref, ba_ref, bx_ref, lam_ref):
    xcb = xc.astype(BF16)
    r = _sigmoid(jnp.dot(xcb, wa_ref[0].astype(BF16), preferred_element_type=F32) + ba_ref[...])
    i = _sigmoid(jnp.dot(xcb, wx_ref[0].astype(BF16), preferred_element_type=F32) + bx_ref[...])
    log_a = -LRU_C * r * _softplus(-lam_ref[...])
    a = jnp.exp(log_a)
    mult = jnp.sqrt(1.0 - a * a)
    return a, mult * (i * xc), _gelu_tanh(gate)


def _lru_prompt_body(xb_ref, gate_ref, cw_ref, cb_ref, wa_ref, wx_ref, ba_ref, bx_ref, lam_ref,
                     y_ref, tail_ref, h_ref, prev_ref, hc_ref):
    c = pl.program_id(2)

    @pl.when(c == 0)
    def _():
        prev_ref[...] = jnp.zeros_like(prev_ref)
        hc_ref[...] = jnp.zeros_like(hc_ref)

    xb = xb_ref[...]
    tt = xb.shape[0]
    prev8 = prev_ref[...]
    row8 = lax.broadcasted_iota(jnp.int32, prev8.shape, 0)
    cw = cw_ref[...]
    xc = cb_ref[...] + xb * cw[CONV_W - 1:CONV_W]
    for j in range(1, CONV_W):
        rolled = pltpu.roll(xb, j, 0)
        head = jnp.where(row8 < j, pltpu.roll(prev8, j, 0), rolled[:SUBLANES])
        xj = jnp.concatenate([head, rolled[SUBLANES:]], axis=0)
        xc = xc + xj * cw[CONV_W - 1 - j:CONV_W - j]

    a, b, gg = _lru_gates(xc, gate_ref[...], wa_ref, wx_ref, ba_ref, bx_ref, lam_ref)

    row = lax.broadcasted_iota(jnp.int32, a.shape, 0)
    s = 1
    while s < tt:
        keep = row >= s
        b = jnp.where(keep, a * pltpu.roll(b, s, 0) + b, b)
        a = jnp.where(keep, a * pltpu.roll(a, s, 0), a)
        s *= 2
    h = a * hc_ref[...] + b

    y_ref[...] = (h * gg).astype(y_ref.dtype)
    tail_ref[0] = xb[tt - SUBLANES:]
    h_ref[0] = h[tt - 1:]
    prev_ref[...] = xb[tt - SUBLANES:]
    hc_ref[...] = h[tt - 1:]


def lru_prompt(proj, p, j, bsz, t_len, tt=256):
    d = proj.shape[1] // 2
    nb, bw = p['lru_w_a'].shape[1], p['lru_w_a'].shape[2]
    tt = min(tt, t_len)
    assert t_len % tt == 0 and tt % SUBLANES == 0
    nc = t_len // tt
    vec = lambda a: a[j].reshape(1, d)
    vspec = pl.BlockSpec((1, bw), lambda b, n, c: (0, n))
    wspec = pl.BlockSpec((1, bw, bw), lambda b, n, c: (n, 0, 0))
    return pl.pallas_call(
        _lru_prompt_body,
        grid=(bsz, nb, nc),
        in_specs=[
            pl.BlockSpec((tt, bw), lambda b, n, c: (b * nc + c, n)),
            pl.BlockSpec((tt, bw), lambda b, n, c: (b * nc + c, nb + n)),
            pl.BlockSpec((CONV_W, bw), lambda b, n, c: (0, n)),
            vspec, wspec, wspec, vspec, vspec, vspec,
        ],
        out_specs=[
            pl.BlockSpec((tt, bw), lambda b, n, c: (b * nc + c, n)),
            pl.BlockSpec((1, SUBLANES, bw), lambda b, n, c: (b, 0, n)),
            pl.BlockSpec((1, 1, bw), lambda b, n, c: (b, 0, n)),
        ],
        out_shape=[
            jax.ShapeDtypeStruct((bsz * t_len, d), BF16),
            jax.ShapeDtypeStruct((bsz, SUBLANES, d), F32),
            jax.ShapeDtypeStruct((bsz, 1, d), F32),
        ],
        scratch_shapes=[pltpu.VMEM((SUBLANES, bw), F32), pltpu.VMEM((1, bw), F32)],
        compiler_params=_params("arbitrary", "arbitrary", "arbitrary"),
    )(proj, proj, p['lru_conv_w'][j], vec(p['lru_conv_b']), p['lru_w_a'][j], p['lru_w_x'][j],
      vec(p['lru_b_a']), vec(p['lru_b_x']), vec(p['lru_lam']))


def _lru_sample_body(xb_ref, gate_ref, cst_ref, h0_ref, cw_ref, cb_ref, wa_ref, wx_ref, ba_ref, bx_ref,
                     lam_ref, y_ref, tail_ref, h_ref, *, t_len, nseq):
    cw = cw_ref[...]
    xs = [cst_ref[q] for q in range(CONV_W - 1)]
    xs += [xb_ref[t * nseq:(t + 1) * nseq] for t in range(t_len)]
    xcs = []
    for t in range(t_len):
        xc = cb_ref[...] + xs[t] * cw[0:1]
        for q in range(1, CONV_W):
            xc = xc + xs[t + q] * cw[q:q + 1]
        xcs.append(xc)
    a, b, gg = _lru_gates(jnp.concatenate(xcs, axis=0), gate_ref[...], wa_ref, wx_ref, ba_ref, bx_ref, lam_ref)
    h = h0_ref[...]
    for t in range(t_len):
        sl = slice(t * nseq, (t + 1) * nseq)
        h = a[sl] * h + b[sl]
        y_ref[sl, :] = (h * gg[sl]).astype(y_ref.dtype)
    h_ref[...] = h
    for q in range(CONV_W - 1):
        tail_ref[q] = xs[t_len + q]


def lru_sample(proj, conv_st, h0, p, j, mp, nseq, t_len):
    d = proj.shape[1] // 2
    nb, bw = p['lru_w_a'].shape[1], p['lru_w_a'].shape[2]
    ms = nseq * t_len
    assert mp % ms == 0
    blk0 = mp // ms
    vec = lambda a: a[j].reshape(1, d)
    vspec = pl.BlockSpec((1, bw), lambda n: (0, n))
    wspec = pl.BlockSpec((1, bw, bw), lambda n: (n, 0, 0))
    return pl.pallas_call(
        functools.partial(_lru_sample_body, t_len=t_len, nseq=nseq),
        grid=(nb,),
        in_specs=[
            pl.BlockSpec((ms, bw), lambda n: (blk0, n)),
            pl.BlockSpec((ms, bw), lambda n: (blk0, nb + n)),
            pl.BlockSpec((CONV_W - 1, nseq, bw), lambda n: (0, 0, n)),
            pl.BlockSpec((nseq, bw), lambda n: (0, n)),
            pl.BlockSpec((CONV_W, bw), lambda n: (0, n)),
            vspec, wspec, wspec, vspec, vspec, vspec,
        ],
        out_specs=[
            pl.BlockSpec((ms, bw), lambda n: (0, n)),
            pl.BlockSpec((CONV_W - 1, nseq, bw), lambda n: (0, 0, n)),
            pl.BlockSpec((nseq, bw), lambda n: (0, n)),
        ],
        out_shape=[
            jax.ShapeDtypeStruct((ms, d), BF16),
            jax.ShapeDtypeStruct((CONV_W - 1, nseq, d), F32),
            jax.ShapeDtypeStruct((nseq, d), F32),
        ],
        compiler_params=_params("arbitrary"),
    )(proj, proj, conv_st, h0, p['lru_conv_w'][j], vec(p['lru_conv_b']), p['lru_w_a'][j], p['lru_w_x'][j],
      vec(p['lru_b_a']), vec(p['lru_b_x']), vec(p['lru_lam']))


def _mix_store(x, prev, mu_ref, out_refs):
    xx = prev - x
    for q, o_ref in enumerate(out_refs):
        o_ref[...] = (x + xx * mu_ref[q:q + 1]).astype(o_ref.dtype)


def _shift_prompt_body(x_ref, mu_ref, *rest, blocks_per_seq):
    out_refs, carry_ref = rest[:-1], rest[-1]
    i = pl.program_id(1)
    x = x_ref[...]
    first = jnp.where(i % blocks_per_seq == 0, jnp.zeros_like(carry_ref[...]), carry_ref[...])
    row = lax.broadcasted_iota(jnp.int32, x.shape, 0)
    prev = jnp.where(row == 0, first, pltpu.roll(x, 1, 0))
    _mix_store(x, prev, mu_ref, out_refs)
    carry_ref[...] = x[x.shape[0] - 1:]


def shift_prompt(x, mu, mp, t_len, tm=256, tc=1024):
    d = x.shape[1]
    tm, tc = min(tm, t_len), min(tc, d)
    assert t_len % tm == 0 and d % tc == 0
    nq = mu.shape[0]
    spec = pl.BlockSpec((tm, tc), lambda c, i: (i, c))
    return pl.pallas_call(
        functools.partial(_shift_prompt_body, blocks_per_seq=t_len // tm),
        grid=(d // tc, mp // tm),
        in_specs=[spec, pl.BlockSpec((nq, tc), lambda c, i: (0, c))],
        out_specs=[spec] * nq,
        out_shape=[jax.ShapeDtypeStruct((mp, d), BF16)] * nq,
        scratch_shapes=[pltpu.VMEM((1, tc), F32)],
        compiler_params=_params("arbitrary", "arbitrary"),
    )(x, mu)


def _shift_sample_body(x_ref, st_ref, mu_ref, *out_refs, nseq):
    x = x_ref[...]
    prev = jnp.concatenate([st_ref[...], x[:x.shape[0] - nseq]], axis=0)
    _mix_store(x, prev, mu_ref, out_refs)


def shift_sample(x, state, mu, mp, nseq, t_len, tc=1024):
    d = x.shape[1]
    tc = min(tc, d)
    ms = nseq * t_len
    blk0 = mp // ms
    nq = mu.shape[0]
    return pl.pallas_call(
        functools.partial(_shift_sample_body, nseq=nseq),
        grid=(d // tc,),
        in_specs=[pl.BlockSpec((ms, tc), lambda c: (blk0, c)),
                  pl.BlockSpec((nseq, tc), lambda c: (0, c)),
                  pl.BlockSpec((nq, tc), lambda c: (0, c))],
        out_specs=[pl.BlockSpec((ms, tc), lambda c: (0, c))] * nq,
        out_shape=[jax.ShapeDtypeStruct((ms, d), BF16)] * nq,
        compiler_params=_params("arbitrary"),
    )(x, state, mu)


def _seg_sum(x):
    lo = lax.broadcasted_iota(jnp.int32, x.shape, x.ndim - 1) < RWKV_HEAD
    s_lo = jnp.sum(jnp.where(lo, x, 0.0), axis=-1, keepdims=True)
    s_hi = jnp.sum(jnp.where(lo, 0.0, x), axis=-1, keepdims=True)
    return jnp.where(lo, s_lo, s_hi)


def _wkv_parse(refs, has_vres, has_state, n_scratch):
    it = iter(refs)
    acts = tuple(next(it) for _ in range(6))
    vres = tuple(next(it) for _ in range(3)) if has_vres else None
    vecs = tuple(next(it) for _ in range(7))
    s0_ref = next(it) if has_state else None
    y_ref, s_ref = next(it), next(it)
    scratch = tuple(next(it) for _ in range(n_scratch))
    return acts, vres, vecs, s0_ref, y_ref, s_ref, scratch


def _wkv_operands(acts, vres, vecs, n_pairs):
    r_ref, k_ref, v_ref, wr_ref, ar_ref, _ = acts
    w0_ref, a0_ref, kk_ref, ka_ref, rk_ref, _, _ = vecs
    r = r_ref[...]
    k = k_ref[...]
    v = v_ref[...]
    w = -_softplus(-(w0_ref[...] + wr_ref[...])) - 0.5
    a = _sigmoid(a0_ref[...] + ar_ref[...])
    if vres is not None:
        vr_ref, vf_ref, v0_ref = vres
        v = v + (vf_ref[...] - v) * _sigmoid(v0_ref[...] + vr_ref[...])
    k2 = k * (1.0 + (a - 1.0) * ka_ref[...])
    kks = []
    for pr in range(n_pairs):
        ls = slice(pr * PAIR, (pr + 1) * PAIR)
        kkr = k[:, ls] * kk_ref[:, ls]
        kks.append(kkr / jnp.maximum(jnp.sqrt(_seg_sum(kkr * kkr)), 1e-12))
    kk = kks[0] if n_pairs == 1 else jnp.concatenate(kks, axis=1)
    return r, v, k2, kk, kk * a, -jnp.exp(w), r * k2 * rk_ref[...]


def _wkv_finish(ys_s, bonus_terms, v, acts, vecs, y_ref, n_pairs):
    g_ref, lg_ref, lb_ref = acts[5], vecs[5], vecs[6]
    for pr in range(n_pairs):
        ls = slice(pr * PAIR, (pr + 1) * PAIR)
        y = ys_s[:, ls]
        mu = _seg_sum(y) * (1.0 / RWKV_HEAD)
        yc = y - mu
        var = _seg_sum(yc * yc) * (1.0 / RWKV_HEAD)
        yn = yc * lax.rsqrt(var + GN_EPS) * lg_ref[:, ls] + lb_ref[:, ls]
        bonus = _seg_sum(bonus_terms[:, ls]) * v[:, ls]
        y_ref[:, ls] = ((yn + bonus) * g_ref[:, ls]).astype(y_ref.dtype)


def _wkv_steps_body(*refs, nb, nt, n_pairs, has_vres):
    acts, vres, vecs, s0_ref, y_ref, s_ref, scratch = _wkv_parse(refs, has_vres, True, 6)
    dec_s, kkn_s, kab_s, k2_s, v_s, ys_s = scratch
    r_ref = acts[0]
    s_ref[...] = s0_ref[...]
    _, v, k2, kk, kka, logdec, bonus_terms = _wkv_operands(acts, vres, vecs, n_pairs)
    dec_s[...] = jnp.exp(logdec)
    kkn_s[...] = -kk
    kab_s[...] = kka
    k2_s[...] = k2
    v_s[...] = v

    eye = (lax.broadcasted_iota(jnp.int32, (RWKV_HEAD, PAIR), 1) % RWKV_HEAD
           == lax.broadcasted_iota(jnp.int32, (RWKV_HEAD, PAIR), 0))
    sub8 = lax.broadcasted_iota(jnp.int32, (SUBLANES, PAIR), 0)
    operand_refs = (kkn_s, v_s, dec_s, kab_s, k2_s, r_ref)

    half_sum = ((lax.broadcasted_iota(jnp.int32, (2 * PAIR, PAIR), 0) % PAIR) // RWKV_HEAD
                == lax.broadcasted_iota(jnp.int32, (2 * PAIR, PAIR), 1) // RWKV_HEAD).astype(BF16)

    def seg_sums(xs):
        cat = jnp.concatenate(xs, axis=0)
        hi = cat.astype(BF16)
        lo = (cat - hi.astype(F32)).astype(BF16)
        out = jnp.dot(jnp.concatenate([hi, lo], axis=1), half_sum, preferred_element_type=F32)
        return [out[q * RWKV_HEAD:(q + 1) * RWKV_HEAD] for q in range(len(xs))]

    chains = range(SUBLANES)
    for t in range(nt):
        def seq_group(gi, carry):
            b0 = gi * SUBLANES
            rows = pl.ds(pl.multiple_of(t * nb + b0, SUBLANES), SUBLANES)
            for pr in range(n_pairs):
                ls = slice(pr * PAIR, (pr + 1) * PAIR)
                kkn, vv, dec, kab, k2r, rr = (ref[rows, ls] for ref in operand_refs)
                s = [s_ref[b0 + u, pr] for u in chains]
                sums = seg_sums([s[u] * kkn[u:u + 1] for u in chains]
                                + [jnp.where(eye, vv[u:u + 1], 0.0) for u in chains])
                s = [s[u] * dec[u:u + 1] + sums[u] * kab[u:u + 1] + sums[SUBLANES + u] * k2r[u:u + 1]
                     for u in chains]
                ycol = seg_sums([s[u] * rr[u:u + 1] for u in chains])
                ytile = jnp.zeros((SUBLANES, PAIR), F32)
                for u in chains:
                    s_ref[b0 + u, pr] = s[u]
                    yrow = jnp.sum(jnp.where(eye, ycol[u], 0.0), axis=0, keepdims=True)
                    ytile = jnp.where(sub8 == u, yrow, ytile)
                ys_s[rows, ls] = ytile
            return carry

        lax.fori_loop(0, nb // SUBLANES, seq_group, 0)

    _wkv_finish(ys_s, bonus_terms, v, acts, vecs, y_ref, n_pairs)


def _bdot(a, b, dims=(((1,), (0,)), ((), ()))):
    return lax.dot_general(a.astype(BF16), b.astype(BF16), dims, preferred_element_type=F32)


_NT = (((1,), (1,)), ((), ()))
_TN = (((0,), (0,)), ((), ()))


def _wkv_chunk_body(*refs, n_pairs, has_vres, chunk):
    acts, vres, vecs, _, y_ref, z_ref, scratch = _wkv_parse(refs, has_vres, False, 6)
    ld_s, kk_s, ka_s, k2_s, v_s, ys_s = scratch
    r_ref = acts[0]

    @pl.when(pl.program_id(2) == 0)
    def _():
        z_ref[...] = jnp.zeros_like(z_ref)

    _, v, k2, kk, kka, logdec, bonus_terms = _wkv_operands(acts, vres, vecs, n_pairs)
    ld_s[...] = logdec
    kk_s[...] = kk
    ka_s[...] = kka
    k2_s[...] = k2
    v_s[...] = v

    row_c = lax.broadcasted_iota(jnp.int32, (chunk, PAIR), 0)
    lo_c = lax.broadcasted_iota(jnp.int32, (chunk, PAIR), 1) < RWKV_HEAD
    i2 = lax.broadcasted_iota(jnp.int32, (2 * chunk, 2 * chunk), 0)
    j2 = lax.broadcasted_iota(jnp.int32, (2 * chunk, 2 * chunk), 1)
    same = (i2 // chunk) == (j2 // chunk)
    strict = same & ((i2 % chunk) > (j2 % chunk))
    incl = same & ((i2 % chunk) >= (j2 % chunk))
    ik = lax.broadcasted_iota(jnp.int32, (PAIR, PAIR), 0)
    jk = lax.broadcasted_iota(jnp.int32, (PAIR, PAIR), 1)
    n2 = 2 * chunk

    def stack2(x):
        return jnp.concatenate([jnp.where(lo_c, x, 0.0), jnp.where(lo_c, 0.0, x)], axis=0)

    def chunk_step(ci, carry):
        rows = pl.ds(pl.multiple_of(ci * chunk, chunk), chunk)
        pairs = range(n_pairs)
        lss = [slice(pr * PAIR, (pr + 1) * PAIR) for pr in pairs]
        qk, qr, kcat, kag, kkg, vb, g_col = ([] for _ in range(7))
        for ls in lss:
            ld = ld_s[rows, ls]
            cum = ld
            s = 1
            while s < chunk:
                cum = cum + jnp.where(row_c >= s, pltpu.roll(cum, s, 0), 0.0)
                s *= 2
            lwc = cum[chunk - 1:chunk]
            kk_c, ka_c, k2_c = kk_s[rows, ls], ka_s[rows, ls], k2_s[rows, ls]
            e_inv = jnp.exp(-cum)
            e_end = jnp.exp(lwc - cum)
            qk.append(stack2(kk_c * jnp.exp(cum - ld)).astype(BF16))
            qr.append(stack2(r_ref[rows, ls] * jnp.exp(cum)))
            kcat.append(jnp.concatenate([stack2(ka_c * e_inv), stack2(k2_c * e_inv)], axis=0).astype(BF16))
            kag.append(stack2(ka_c * e_end).astype(BF16))
            kkg.append(stack2(k2_c * e_end).astype(BF16))
            vb.append(stack2(v_s[rows, ls]).astype(BF16))
            g_col.append(jnp.sum(jnp.where(ik == jk, jnp.exp(lwc), 0.0), axis=1, keepdims=True))
        sc = [_bdot(jnp.concatenate([qk[p], qr[p].astype(BF16)], axis=0), kcat[p], _NT) for p in pairs]
        a_ak = [jnp.where(strict, sc[p][:n2, :n2], 0.0) for p in pairs]
        a_kk = [jnp.where(strict, sc[p][:n2, n2:], 0.0).astype(BF16) for p in pairs]
        b_ak = [jnp.where(incl, sc[p][n2:, :n2], 0.0).astype(BF16) for p in pairs]
        b_kk = [jnp.where(incl, sc[p][n2:, n2:], 0.0).astype(BF16) for p in pairs]
        tinv = [jnp.where(i2 == j2, 1.0, 0.0) - a for a in a_ak]
        apow = [a.astype(BF16) for a in a_ak]
        s = 1
        while 2 * s < chunk:
            apow = [_bdot(a, a).astype(BF16) for a in apow]
            tinv = [t + _bdot(t, a) for t, a in zip(tinv, apow)]
            s *= 2
        w = [_bdot(a_kk[p], vb[p]).astype(BF16) for p in pairs]
        pu = [(-_bdot(tinv[p], jnp.concatenate([qk[p], w[p]], axis=1))).astype(BF16) for p in pairs]
        bp = [_bdot(b_ak[p], pu[p]) for p in pairs]
        kp = [_bdot(kag[p], pu[p], _TN) for p in pairs]
        y0 = [bp[p][:, PAIR:] + _bdot(b_kk[p], vb[p]) for p in pairs]
        z0 = [kp[p][:, PAIR:] + _bdot(kkg[p], vb[p], _TN) for p in pairs]
        for p in pairs:
            z = z_ref[0, p]
            zb = z.astype(BF16)
            yb = _bdot(qr[p] + bp[p][:, :PAIR], zb) + y0[p]
            ys_s[rows, lss[p]] = yb[:chunk] + yb[chunk:]
            z_ref[0, p] = g_col[p] * z + _bdot(kp[p][:, :PAIR], zb) + z0[p]
        return carry

    lax.fori_loop(0, r_ref.shape[0] // chunk, chunk_step, 0)
    _wkv_finish(ys_s, bonus_terms, v, acts, vecs, y_ref, n_pairs)


def _wkv_call(body, acts, vres, p, j, state, *, grid, rows, n_pairs, act_map, out_map, vec_map, st_map,
              out_rows, state_shape, state_block):
    d = acts[0].shape[1]
    lw = n_pairs * PAIR
    aspec = pl.BlockSpec((rows, lw), act_map)
    vspec = pl.BlockSpec((1, lw), vec_map)
    sspec = pl.BlockSpec(state_block, st_map)
    vec = lambda a: a.reshape(1, d)
    args, in_specs = list(acts), [aspec] * 6
    if vres is not None:
        v_raw, v_first, v0 = vres
        args += [v_raw, v_first, vec(v0)]
        in_specs += [aspec, aspec, vspec]
    args += [vec(p['rw_w0'][j]), vec(p['rw_a0'][j]), vec(p['rw_k_k'][j]), vec(p['rw_k_a'][j]),
             vec(p['rw_r_k'][j]), vec(p['rw_lnx_g'][j]), vec(p['rw_lnx_b'][j])]
    in_specs += [vspec] * 7
    if state is not None:
        args.append(state)
        in_specs.append(sspec)
    return pl.pallas_call(
        body,
        grid=grid,
        in_specs=in_specs,
        out_specs=[pl.BlockSpec((rows, lw), out_map), sspec],
        out_shape=[jax.ShapeDtypeStruct((out_rows, d), BF16), jax.ShapeDtypeStruct(state_shape, F32)],
        scratch_shapes=[pltpu.VMEM((rows, lw), F32)] * 6,
        compiler_params=_params(*(["arbitrary"] * len(grid))),
    )(*args)


def wkv_prompt(acts, vres, p, j, bsz, t_len, n_pairs=8, rows=256, chunk=64):
    d = acts[0].shape[1]
    n_pairs = min(n_pairs, d // PAIR)
    rows = min(rows, t_len)
    assert t_len % rows == 0 and rows % chunk == 0
    nc = t_len // rows
    return _wkv_call(
        functools.partial(_wkv_chunk_body, n_pairs=n_pairs, has_vres=vres is not None, chunk=chunk),
        acts, vres, p, j, None, grid=(bsz, d // (n_pairs * PAIR), nc), rows=rows, n_pairs=n_pairs,
        act_map=lambda b, l, c: (b * nc + c, l), out_map=lambda b, l, c: (b * nc + c, l),
        vec_map=lambda b, l, c: (0, l), st_map=lambda b, l, c: (b, l, 0, 0),
        out_rows=bsz * t_len, state_shape=(bsz, d // PAIR, PAIR, PAIR), state_block=(1, n_pairs, PAIR, PAIR))


def wkv_sample(acts, vres, p, j, state, row_blk0, nseq, t_len):
    d = acts[0].shape[1]
    return _wkv_call(
        functools.partial(_wkv_steps_body, nb=nseq, nt=t_len, n_pairs=1, has_vres=vres is not None),
        acts, vres, p, j, state, grid=(d // PAIR,), rows=nseq * t_len, n_pairs=1,
        act_map=lambda l: (row_blk0, l), out_map=lambda l: (0, l), vec_map=lambda l: (0, l),
        st_map=lambda l: (0, l, 0, 0), out_rows=nseq * t_len, state_shape=(nseq, d // PAIR, RWKV_HEAD, PAIR),
        state_block=(nseq, 1, RWKV_HEAD, PAIR))


def _pair_state(s):
    b, h, nv, nk = s.shape
    return s.reshape(b, h // 2, 2, nv, nk).transpose(0, 1, 3, 2, 4).reshape(b, h // 2, nv, 2 * nk)


def _unpair_state(s):
    b, hp, nv, nk2 = s.shape
    return s.reshape(b, hp, nv, 2, nk2 // 2).transpose(0, 1, 3, 2, 4).reshape(b, hp * 2, nv, nk2 // 2)


def _state_from_blockdiag(z):
    b, hp = z.shape[:2]
    z = z.reshape(b, hp, 2, RWKV_HEAD, 2, RWKV_HEAD)
    diag = jnp.stack([z[:, :, 0, :, 0, :], z[:, :, 1, :, 1, :]], axis=2)
    return diag.transpose(0, 1, 2, 4, 3).reshape(b, hp * 2, RWKV_HEAD, RWKV_HEAD)


def _route(logits, n_groups, n_experts, tm):
    m = logits.shape[0]
    epg = n_experts // n_groups
    g_logits = logits[:, :n_groups]
    g_idx = jnp.argmax(g_logits, axis=-1)
    p_group = jnp.max(jax.nn.softmax(g_logits, axis=-1), axis=-1, keepdims=True)
    e_logits = logits[:, n_groups:n_groups + n_experts].reshape(m, n_groups, epg)
    e_in = jnp.take_along_axis(e_logits, g_idx[:, None, None], axis=1)[:, 0]
    top_val, top_idx = lax.top_k(e_in, TOP_K)
    gate = p_group * jax.nn.softmax(top_val, axis=-1)
    expert_flat = (g_idx[:, None] * epg + top_idx).astype(jnp.int32).reshape(-1)

    n_assign = m * TOP_K
    order = jnp.argsort(expert_flat).astype(jnp.int32)
    e_sorted = expert_flat[order]
    tok_sorted = order // TOP_K
    counts = jnp.sum(expert_flat[:, None] == jnp.arange(n_experts, dtype=jnp.int32)[None, :], axis=0, dtype=jnp.int32)
    start = jnp.cumsum(counts) - counts
    nblk_e = (counts + tm - 1) // tm
    blk_end = jnp.cumsum(nblk_e)
    blk_start = blk_end - nblk_e
    slot_sorted = blk_start[e_sorted] * tm + (jnp.arange(n_assign, dtype=jnp.int32) - start[e_sorted])
    slot_of_assign = slot_sorted[jnp.argsort(order)]
    n_blocks = (n_assign + n_experts * (tm - 1) + tm - 1) // tm
    blk = jnp.arange(n_blocks, dtype=jnp.int32)
    blk_expert = jnp.minimum(jnp.sum(blk[:, None] >= blk_end[None, :], axis=1), n_experts - 1)
    e_slot = jnp.repeat(blk_expert, tm)
    off = jnp.arange(n_blocks * tm, dtype=jnp.int32) - blk_start[e_slot] * tm
    valid = (off >= 0) & (off < counts[e_slot])
    slot_tok = jnp.where(valid, tok_sorted[jnp.clip(start[e_slot] + off, 0, n_assign - 1)], 0)
    return slot_tok, slot_of_assign.reshape(m, TOP_K), gate.astype(F32), blk_start, nblk_e


def _gather_body(tok_ref, x_hbm, o_ref, buf_ref, sem, *, tm):
    i = pl.program_id(0)

    def copy(step, r):
        slot = step % 2
        return pltpu.make_async_copy(x_hbm.at[pl.ds(tok_ref[step * tm + r], 1)],
                                     buf_ref.at[slot, pl.ds(r, 1)], sem.at[slot])

    def issue(step):
        lax.fori_loop(0, tm, lambda r, c: (copy(step, r).start(), c)[1], 0)

    @pl.when(i == 0)
    def _():
        issue(0)

    @pl.when(i + 1 < pl.num_programs(0))
    def _():
        issue(i + 1)

    lax.fori_loop(0, tm, lambda r, c: (copy(i, r).wait(), c)[1], 0)
    o_ref[...] = buf_ref[i % 2].astype(o_ref.dtype)


def gather_rows(x, slot_tok, tm=256):
    d = x.shape[1]
    n_slots = slot_tok.shape[0]
    tm = _div_tile(n_slots, tm)
    return pl.pallas_call(
        functools.partial(_gather_body, tm=tm),
        grid_spec=pltpu.PrefetchScalarGridSpec(
            num_scalar_prefetch=1,
            grid=(n_slots // tm,),
            in_specs=[pl.BlockSpec(memory_space=pl.ANY)],
            out_specs=pl.BlockSpec((tm, d), lambda i, tok: (i, 0)),
            scratch_shapes=[pltpu.VMEM((2, tm, d), F32), pltpu.SemaphoreType.DMA((2,))],
        ),
        out_shape=jax.ShapeDtypeStruct((n_slots, d), BF16),
        compiler_params=_params("arbitrary"),
    )(slot_tok, x)


def _expert_blocks(bs_ref, nb_ref, src_hbm, dst_hbm, xbuf, obuf, xsem, osem, compute, *, tm, dst_cols,
                   per_block=None):
    e = pl.program_id(0)
    row0 = bs_ref[e] * tm
    n = nb_ref[e]

    def rows(b):
        return pl.ds(pl.multiple_of(row0 + b * tm, tm), tm)

    def xcopy(b):
        return pltpu.make_async_copy(src_hbm.at[rows(b)], xbuf.at[b % 2], xsem.at[b % 2])

    def ocopy(b):
        dst = dst_hbm.at[rows(b)] if dst_cols is None else dst_hbm.at[rows(b), dst_cols]
        return pltpu.make_async_copy(obuf.at[b % 2], dst, osem.at[b % 2])

    @pl.when(n > 0)
    def _():
        xcopy(0).start()

    def block(b, c):
        xcopy(b).wait()

        @pl.when(b + 1 < n)
        def _():
            xcopy(b + 1).start()

        if per_block is not None:
            per_block(b)

        @pl.when(b >= 2)
        def _():
            ocopy(b - 2).wait()

        obuf[b % 2] = compute(xbuf[b % 2]).astype(obuf.dtype)
        ocopy(b).start()
        return c

    lax.fori_loop(0, n, block, 0)

    @pl.when(n >= 2)
    def _():
        ocopy(n - 2).wait()

    @pl.when(n >= 1)
    def _():
        ocopy(n - 1).wait()

    @pl.when(e == pl.num_programs(0) - 1)
    def _():
        obuf[0] = jnp.zeros(obuf.shape[1:], obuf.dtype)

        def fill(blk, c):
            fill_rows = pl.ds(pl.multiple_of(blk * tm, tm), tm)
            dst = dst_hbm.at[fill_rows] if dst_cols is None else dst_hbm.at[fill_rows, dst_cols]
            cp = pltpu.make_async_copy(obuf.at[0], dst, osem.at[0])
            cp.start()
            cp.wait()
            return c

        lax.fori_loop(bs_ref[e] + n, dst_hbm.shape[0] // tm, fill, 0)


def _ffn_up_body(bs_ref, nb_ref, x_hbm, wg_hbm, wu_hbm, o_hbm, wbuf, wgbf_ref, wubf_ref, xbuf, obuf,
                 xsem, osem, wsem, *, tm, tn, layer):
    e, j, n_j = pl.program_id(0), pl.program_id(1), pl.num_programs(1)
    step = e * n_j + j
    has_next = step + 1 < pl.num_programs(0) * n_j
    rows_p = wbuf.shape[2] // N_WEIGHT_PIECES

    def wcopies(s, piece):
        rows = pl.ds(pl.multiple_of(piece * rows_p, rows_p), rows_p)
        cols = pl.ds(pl.multiple_of((s % n_j) * tn, tn), tn)
        return [pltpu.make_async_copy(w_hbm.at[layer, s // n_j, rows, cols], wbuf.at[s % 2, q, rows], wsem.at[s % 2])
                for q, w_hbm in enumerate((wg_hbm, wu_hbm))]

    def issue_piece(s, piece):
        for cp in wcopies(s, piece):
            cp.start()

    @pl.when(step == 0)
    def _():
        for piece in range(N_WEIGHT_PIECES):
            issue_piece(0, piece)

    for piece in range(N_WEIGHT_PIECES):
        for cp in wcopies(step, piece):
            cp.wait()
    wgbf_ref[...] = wbuf[step % 2, 0].astype(BF16)
    wubf_ref[...] = wbuf[step % 2, 1].astype(BF16)

    def compute(x):
        hg = jnp.dot(x, wgbf_ref[...], preferred_element_type=F32)
        hu = jnp.dot(x, wubf_ref[...], preferred_element_type=F32)
        return hg * _sigmoid(hg) * hu

    def per_block(b):
        @pl.when(has_next & (b < N_WEIGHT_PIECES))
        def _():
            issue_piece(step + 1, b)

    cols = pl.ds(pl.multiple_of(j * tn, tn), tn)
    _expert_blocks(bs_ref, nb_ref, x_hbm, o_hbm, xbuf, obuf, xsem, osem, compute, tm=tm, dst_cols=cols,
                   per_block=per_block)

    @pl.when(has_next)
    def _():
        lax.fori_loop(jnp.minimum(nb_ref[e], N_WEIGHT_PIECES), N_WEIGHT_PIECES,
                      lambda piece, c: (issue_piece(step + 1, piece), c)[1], 0)


def _ffn_down_body(bs_ref, nb_ref, h_hbm, wd_ref, o_hbm, wdbf_ref, xbuf, obuf, xsem, osem, *, tm):
    wdbf_ref[...] = wd_ref[...].astype(BF16)

    def compute(h):
        return jnp.dot(h, wdbf_ref[...], preferred_element_type=F32)

    _expert_blocks(bs_ref, nb_ref, h_hbm, o_hbm, xbuf, obuf, xsem, osem, compute, tm=tm, dst_cols=None)


def expert_ffn(x_sorted, blk_start, nblk_e, w_gate, w_up, w_down, layer, tm, tn_up=512):
    n_slots, d = x_sorted.shape
    n_experts, de = w_gate.shape[1], w_gate.shape[3]
    tn_up = min(tn_up, de)
    any_spec = pl.BlockSpec(memory_space=pl.ANY)
    dma2 = pltpu.SemaphoreType.DMA((2,))
    hid = pl.pallas_call(
        functools.partial(_ffn_up_body, tm=tm, tn=tn_up, layer=layer),
        grid_spec=pltpu.PrefetchScalarGridSpec(
            num_scalar_prefetch=2,
            grid=(n_experts, de // tn_up),
            in_specs=[any_spec, any_spec, any_spec],
            out_specs=any_spec,
            scratch_shapes=[pltpu.VMEM((2, 2, d, tn_up), F32),
                            pltpu.VMEM((d, tn_up), BF16), pltpu.VMEM((d, tn_up), BF16),
                            pltpu.VMEM((2, tm, d), BF16), pltpu.VMEM((2, tm, tn_up), BF16), dma2, dma2, dma2],
        ),
        out_shape=jax.ShapeDtypeStruct((n_slots, de), BF16),
        compiler_params=_params("arbitrary", "arbitrary"),
    )(blk_start, nblk_e, x_sorted, w_gate, w_up)
    return pl.pallas_call(
        functools.partial(_ffn_down_body, tm=tm),
        grid_spec=pltpu.PrefetchScalarGridSpec(
            num_scalar_prefetch=2,
            grid=(n_experts,),
            in_specs=[any_spec, pl.BlockSpec((None, None, de, d), lambda e, bs, nb: (layer, e, 0, 0))],
            out_specs=any_spec,
            scratch_shapes=[pltpu.VMEM((de, d), BF16), pltpu.VMEM((2, tm, de), BF16),
                            pltpu.VMEM((2, tm, d), F32), dma2, dma2],
        ),
        out_shape=jax.ShapeDtypeStruct((n_slots, d), F32),
        compiler_params=_params("arbitrary"),
    )(blk_start, nblk_e, hid, w_down)


def _combine_ln_body(slot_ref, x_ref, gate_ref, g_ref, b_ref, y_hbm, o_ref, obf_ref, buf_ref, sem, *, tm, alpha):
    i = pl.program_id(0)

    def copy(step, q):
        slot = step % 2
        return pltpu.make_async_copy(y_hbm.at[pl.ds(slot_ref[step * tm * TOP_K + q], 1)],
                                     buf_ref.at[slot, q % TOP_K, pl.ds(q // TOP_K, 1)], sem.at[slot])

    def issue(step):
        lax.fori_loop(0, tm * TOP_K, lambda q, c: (copy(step, q).start(), c)[1], 0)

    @pl.when(i == 0)
    def _():
        issue(0)

    @pl.when(i + 1 < pl.num_programs(0))
    def _():
        issue(i + 1)

    lax.fori_loop(0, tm * TOP_K, lambda q, c: (copy(i, q).wait(), c)[1], 0)
    gate = gate_ref[...]
    buf = buf_ref.at[i % 2]
    ffn = buf[0] * gate[:, 0:1]
    for q in range(1, TOP_K):
        ffn = ffn + buf[q] * gate[:, q:q + 1]
    y = _layer_norm_rows(alpha * x_ref[...] + ffn, g_ref[...], b_ref[...])
    o_ref[...] = y
    obf_ref[...] = y.astype(BF16)


def combine_layer_norm(x, out_slots, slot_of_assign, gate, g, b, alpha, tm=256):
    m, d = x.shape
    tm = _div_tile(m, tm)
    row = pl.BlockSpec((tm, d), lambda i, s: (i, 0))
    vec = pl.BlockSpec((1, d), lambda i, s: (0, 0))
    return pl.pallas_call(
        functools.partial(_combine_ln_body, tm=tm, alpha=alpha),
        grid_spec=pltpu.PrefetchScalarGridSpec(
            num_scalar_prefetch=1,
            grid=(m // tm,),
            in_specs=[row, pl.BlockSpec((tm, TOP_K), lambda i, s: (i, 0)), vec, vec,
                      pl.BlockSpec(memory_space=pl.ANY)],
            out_specs=[row, row],
            scratch_shapes=[pltpu.VMEM((2, TOP_K, tm, d), F32), pltpu.SemaphoreType.DMA((2,))],
        ),
        out_shape=[jax.ShapeDtypeStruct((m, d), F32), jax.ShapeDtypeStruct((m, d), BF16)],
        compiler_params=_params("arbitrary"),
    )(slot_of_assign.reshape(-1), x, gate, g.reshape(1, d), b.reshape(1, d), out_slots)


def _trunk(x_prompt, x_sample, state_conv, state_h, state_shift, state_wkv, p, moe_tm=128):
    bsz, t_len, d = x_prompt.shape
    nseq, dec_t, _ = x_sample.shape
    mp, ms = bsz * t_len, nseq * dec_t
    depth = p['ln_g'].shape[0]
    alpha = (2 * depth) ** 0.25
    n_groups = p['moe_w_group'].shape[2]
    n_experts = p['moe_w_expert'].shape[2]

    x = jnp.concatenate([x_prompt.reshape(mp, d), x_sample.transpose(1, 0, 2).reshape(ms, d)], axis=0)
    xbf = x.astype(BF16)
    conv_s_tm = state_conv.transpose(0, 2, 1, 3)

    conv_p, h_p, shift_p, wkv_p, conv_s, h_s, shift_s, wkv_s = ([] for _ in range(8))
    v_first = None
    for i in range(depth):
        j = i // 2
        if i % 2 == 0:
            proj = matmul(xbf, p['lru_w_in'], j, mp, ms)
            y_p, tail_p, hl_p = lru_prompt(proj, p, j, bsz, t_len)
            y_s, tail_s, hl_s = lru_sample(proj, conv_s_tm[j], state_h[j], p, j, mp, nseq, dec_t)
            mix = matmul((y_p, y_s), p['lru_w_out'], j, mp, ms)
            conv_p.append(tail_p[:, SUBLANES - (CONV_W - 1):])
            h_p.append(hl_p[:, 0])
            conv_s.append(tail_s.transpose(1, 0, 2))
            h_s.append(hl_s)
        else:
            mu = p['rw_mu'][j]
            mixed_p = shift_prompt(x, mu, mp, t_len)
            mixed_s = shift_sample(x, state_shift[j], mu, mp, nseq, dec_t)
            xr, xw, xk, xv, xa, xg = zip(mixed_p, mixed_s)
            r = matmul(xr, p['rw_w_r'], j, mp, ms)
            k = matmul(xk, p['rw_w_k'], j, mp, ms)
            v = matmul(xv, p['rw_w_v'], j, mp, ms)
            w_raw = lora(xw, p['rw_w1'][j], p['rw_w2'][j], mp, ms, "tanh")
            a_raw = lora(xa, p['rw_a1'][j], p['rw_a2'][j], mp, ms, "none")
            g = lora(xg, p['rw_g1'][j], p['rw_g2'][j], mp, ms, "sigmoid")
            if j == 0:
                vres, v_first = None, v
            else:
                v_raw = lora(xv, p['rw_v1'][j - 1], p['rw_v2'][j - 1], mp, ms, "none")
                vres = (v_raw, v_first, p['rw_v0'][j - 1])
            acts = (r, k, v, w_raw, a_raw, g)
            yg_p, z_p = wkv_prompt(acts, vres, p, j, bsz, t_len)
            yg_s, st_s = wkv_sample(acts, vres, p, j, _pair_state(state_wkv[j]), mp // ms, nseq, dec_t)
            mix = matmul((yg_p, yg_s), p['rw_w_o'], j, mp, ms)
            shift_p.append(x[t_len - 1:mp:t_len])
            shift_s.append(x[mp + (dec_t - 1) * nseq:])
            wkv_p.append(_state_from_blockdiag(z_p))
            wkv_s.append(_unpair_state(st_s))

        w_route = jnp.concatenate([p['moe_w_group'][i], p['moe_w_expert'][i]], axis=1)
        b_route = jnp.concatenate([p['moe_b_group'][i], p['moe_b_expert'][i]], axis=0)
        pad = ROUTE_PAD - w_route.shape[1]
        w_route = jnp.pad(w_route, ((0, 0), (0, pad)))
        b_route = jnp.pad(b_route, (0, pad)).reshape(1, ROUTE_PAD)
        x, xbf, logits = residual_layer_norm(x, mix, p['ln_g'][i, 0], p['ln_b'][i, 0], alpha,
                                             router=(w_route, b_route))
        slot_tok, slot_of_assign, gate, blk_start, nblk_e = _route(logits, n_groups, n_experts, moe_tm)
        x_sorted = gather_rows(x, slot_tok)
        out_slots = expert_ffn(x_sorted, blk_start, nblk_e, p['moe_w_gate'], p['moe_w_up'], p['moe_w_down'],
                               i, moe_tm)
        x, xbf = combine_layer_norm(x, out_slots, slot_of_assign, gate, p['ln_g'][i, 1], p['ln_b'][i, 1], alpha)

    y_prompt = x[:mp].reshape(bsz, t_len, d)
    y_sample = x[mp:].reshape(dec_t, nseq, d).transpose(1, 0, 2)
    return (y_prompt, y_sample, jnp.stack(conv_p), jnp.stack(h_p), jnp.stack(shift_p), jnp.stack(wkv_p),
            jnp.stack(conv_s), jnp.stack(h_s), jnp.stack(shift_s), jnp.stack(wkv_s))


def kernel(x_prompt, x_sample, state_conv, state_h, state_shift, state_wkv, ln_g, ln_b, lru_w_in, lru_conv_w, lru_conv_b, lru_w_a, lru_b_a, lru_w_x, lru_b_x, lru_lam, lru_w_out, rw_mu, rw_w_r, rw_w_k, rw_w_v, rw_w_o, rw_w0, rw_w1, rw_w2, rw_a0, rw_a1, rw_a2, rw_v0, rw_v1, rw_v2, rw_g1, rw_g2, rw_k_k, rw_k_a, rw_r_k, rw_lnx_g, rw_lnx_b, moe_w_group, moe_b_group, moe_w_expert, moe_b_expert, moe_w_gate, moe_w_up, moe_w_down):
    p = dict(ln_g=ln_g, ln_b=ln_b, lru_w_in=lru_w_in, lru_conv_w=lru_conv_w, lru_conv_b=lru_conv_b,
             lru_w_a=lru_w_a, lru_b_a=lru_b_a, lru_w_x=lru_w_x, lru_b_x=lru_b_x, lru_lam=lru_lam,
             lru_w_out=lru_w_out, rw_mu=rw_mu, rw_w_r=rw_w_r, rw_w_k=rw_w_k, rw_w_v=rw_w_v, rw_w_o=rw_w_o,
             rw_w0=rw_w0, rw_w1=rw_w1, rw_w2=rw_w2, rw_a0=rw_a0, rw_a1=rw_a1, rw_a2=rw_a2,
             rw_v0=rw_v0, rw_v1=rw_v1, rw_v2=rw_v2, rw_g1=rw_g1, rw_g2=rw_g2, rw_k_k=rw_k_k, rw_k_a=rw_k_a,
             rw_r_k=rw_r_k, rw_lnx_g=rw_lnx_g, rw_lnx_b=rw_lnx_b, moe_w_group=moe_w_group,
             moe_b_group=moe_b_group, moe_w_expert=moe_w_expert, moe_b_expert=moe_b_expert,
             moe_w_gate=moe_w_gate, moe_w_up=moe_w_up, moe_w_down=moe_w_down)
    return _trunk(x_prompt, x_sample, state_conv, state_h, state_shift, state_wkv, p)
```

```python
import functools

import jax
import jax.numpy as jnp
from jax import lax
from jax.experimental import pallas as pl
from jax.experimental.pallas import tpu as pltpu

F32 = jnp.float32
BF16 = jnp.bfloat16

V7X_VMEM_LIMIT_BYTES = 56 * 1024 * 1024
LANES = 128
SUBLANES = 8
RWKV_HEAD = 64
PAIR = 2 * RWKV_HEAD
CONV_W = 4
LRU_C = 8.0
GN_EPS = 64e-5
LN_EPS = 1e-5
TOP_K = 2
ROUTE_PAD = LANES


def _params(*sem):
    return pltpu.CompilerParams(dimension_semantics=sem, vmem_limit_bytes=V7X_VMEM_LIMIT_BYTES)


def _sigmoid(x):
    return 1.0 / (1.0 + jnp.exp(-x))


def _softplus(x):
    return jnp.maximum(x, 0.0) + jnp.log1p(jnp.exp(-jnp.abs(x)))


def _gelu_tanh(x):
    return 0.5 * x * (1.0 + jnp.tanh(0.7978845608028654 * (x + 0.044715 * (x * x * x))))


def _row_tile(mp, ms, cap):
    tm = min(cap, ms)
    assert mp % tm == 0 and ms % tm == 0 and tm % SUBLANES == 0, (mp, ms, tm)
    return tm


def _div_tile(m, cap):
    tm = min(cap, m) // SUBLANES * SUBLANES
    while m % tm:
        tm -= SUBLANES
    return tm


def _mm_body(xp_ref, xs_ref, w_ref, o_ref, wbf_ref, *, n_p):
    i = pl.program_id(1)

    @pl.when(i == 0)
    def _():
        wbf_ref[...] = w_ref[...].astype(BF16)

    def run(x_ref):
        o_ref[...] = jnp.dot(x_ref[...], wbf_ref[...], preferred_element_type=F32).astype(o_ref.dtype)

    @pl.when(i < n_p)
    def _():
        run(xp_ref)

    @pl.when(i >= n_p)
    def _():
        run(xs_ref)


def _split_rows(act, mp):
    if isinstance(act, tuple):
        return act[0], act[1], 0
    return act, act, mp


def matmul(act, w, layer, mp, ms, out_dtype=F32, tn=512, tm_cap=512):
    xp, xs, s_off = _split_rows(act, mp)
    _, k, n = w.shape
    tn = min(tn, n)
    assert n % tn == 0
    tm = _row_tile(mp, ms, tm_cap)
    n_p, n_s = mp // tm, ms // tm
    s_blk0 = s_off // tm
    return pl.pallas_call(
        functools.partial(_mm_body, n_p=n_p),
        grid=(n // tn, n_p + n_s),
        in_specs=[
            pl.BlockSpec((tm, k), lambda j, i: (jnp.minimum(i, n_p - 1), 0)),
            pl.BlockSpec((tm, k), lambda j, i: (s_blk0 + jnp.maximum(i - n_p, 0), 0)),
            pl.BlockSpec((None, k, tn), lambda j, i: (layer, 0, j)),
        ],
        out_specs=pl.BlockSpec((tm, tn), lambda j, i: (i, j)),
        out_shape=jax.ShapeDtypeStruct((mp + ms, n), out_dtype),
        scratch_shapes=[pltpu.VMEM((k, tn), BF16)],
        compiler_params=_params("arbitrary", "arbitrary"),
    )(xp, xs, w)


def _lora_body(xp_ref, xs_ref, w1_ref, w2_ref, o_ref, w1bf_ref, w2bf_ref, *, n_p, act):
    i = pl.program_id(0)

    @pl.when(i == 0)
    def _():
        w1bf_ref[...] = w1_ref[...].astype(BF16)
        w2bf_ref[...] = w2_ref[...].astype(BF16)

    def run(x_ref):
        t = jnp.dot(x_ref[...], w1bf_ref[...], preferred_element_type=F32)
        if act == "tanh":
            t = jnp.tanh(t)
        elif act == "sigmoid":
            t = _sigmoid(t)
        o_ref[...] = jnp.dot(t.astype(BF16), w2bf_ref[...], preferred_element_type=F32)

    @pl.when(i < n_p)
    def _():
        run(xp_ref)

    @pl.when(i >= n_p)
    def _():
        run(xs_ref)


def lora(act_in, w1, w2, mp, ms, act, tm_cap=256):
    xp, xs, s_off = _split_rows(act_in, mp)
    pad = -w1.shape[1] % LANES
    w1 = jnp.pad(w1, ((0, 0), (0, pad)))
    w2 = jnp.pad(w2, ((0, pad), (0, 0)))
    k, r = w1.shape
    n = w2.shape[1]
    tm = _row_tile(mp, ms, tm_cap)
    n_p, n_s = mp // tm, ms // tm
    s_blk0 = s_off // tm
    return pl.pallas_call(
        functools.partial(_lora_body, n_p=n_p, act=act),
        grid=(n_p + n_s,),
        in_specs=[
            pl.BlockSpec((tm, k), lambda i: (jnp.minimum(i, n_p - 1), 0)),
            pl.BlockSpec((tm, k), lambda i: (s_blk0 + jnp.maximum(i - n_p, 0), 0)),
            pl.BlockSpec((k, r), lambda i: (0, 0)),
            pl.BlockSpec((r, n), lambda i: (0, 0)),
        ],
        out_specs=pl.BlockSpec((tm, n), lambda i: (i, 0)),
        out_shape=jax.ShapeDtypeStruct((mp + ms, n), F32),
        scratch_shapes=[pltpu.VMEM((k, r), BF16), pltpu.VMEM((r, n), BF16)],
        compiler_params=_params("arbitrary"),
    )(xp, xs, w1, w2)


def _layer_norm_rows(z, g, b):
    mu = jnp.mean(z, axis=-1, keepdims=True)
    zc = z - mu
    var = jnp.mean(zc * zc, axis=-1, keepdims=True)
    return zc * lax.rsqrt(var + LN_EPS) * g + b


def _ln_body(x_ref, mix_ref, g_ref, b_ref, *rest, alpha, with_router):
    if with_router:
        wr_ref, br_ref, o_ref, obf_ref, lg_ref = rest
    else:
        o_ref, obf_ref = rest
    y = _layer_norm_rows(alpha * x_ref[...] + mix_ref[...], g_ref[...], b_ref[...])
    ybf = y.astype(BF16)
    o_ref[...] = y
    obf_ref[...] = ybf
    if with_router:
        lg_ref[...] = jnp.dot(ybf, wr_ref[...].astype(BF16), preferred_element_type=F32) + br_ref[...]


def residual_layer_norm(x, mix, g, b, alpha, router=None, tm=256):
    m, d = x.shape
    tm = _div_tile(m, tm)
    row = pl.BlockSpec((tm, d), lambda i: (i, 0))
    vec = pl.BlockSpec((1, d), lambda i: (0, 0))
    in_specs = [row, row, vec, vec]
    args = [x, mix, g.reshape(1, d), b.reshape(1, d)]
    out_specs = [row, row]
    out_shape = [jax.ShapeDtypeStruct((m, d), F32), jax.ShapeDtypeStruct((m, d), BF16)]
    if router is not None:
        wr, br = router
        in_specs += [pl.BlockSpec((d, ROUTE_PAD), lambda i: (0, 0)), pl.BlockSpec((1, ROUTE_PAD), lambda i: (0, 0))]
        args += [wr, br]
        out_specs.append(pl.BlockSpec((tm, ROUTE_PAD), lambda i: (i, 0)))
        out_shape.append(jax.ShapeDtypeStruct((m, ROUTE_PAD), F32))
    return pl.pallas_call(
        functools.partial(_ln_body, alpha=alpha, with_router=router is not None),
        grid=(m // tm,),
        in_specs=in_specs,
        out_specs=out_specs,
        out_shape=out_shape,
        compiler_params=_params("arbitrary"),
    )(*args)


def _lru_gates(xc, gate, wa_ref, wx_ref, ba_ref, bx_ref, lam_ref):
    xcb = xc.astype(BF16)
    r = _sigmoid(jnp.dot(xcb, wa_ref[0].astype(BF16), preferred_element_type=F32) + ba_ref[...])
    i = _sigmoid(jnp.dot(xcb, wx_ref[0].astype(BF16), preferred_element_type=F32) + bx_ref[...])
    log_a = -LRU_C * r * _softplus(-lam_ref[...])
    a = jnp.exp(log_a)
    mult = jnp.sqrt(1.0 - a * a)
    return a, mult * (i * xc), _gelu_tanh(gate)


def _lru_prompt_body(xb_ref, gate_ref, cw_ref, cb_ref, wa_ref, wx_ref, ba_ref, bx_ref, lam_ref,
                     y_ref, tail_ref, h_ref, prev_ref, hc_ref):
    c = pl.program_id(2)

    @pl.when(c == 0)
    def _():
        prev_ref[...] = jnp.zeros_like(prev_ref)
        hc_ref[...] = jnp.zeros_like(hc_ref)

    xb = xb_ref[...]
    tt = xb.shape[0]
    prev8 = prev_ref[...]
    row8 = lax.broadcasted_iota(jnp.int32, prev8.shape, 0)
    cw = cw_ref[...]
    xc = cb_ref[...] + xb * cw[CONV_W - 1:CONV_W]
    for j in range(1, CONV_W):
        rolled = pltpu.roll(xb, j, 0)
        head = jnp.where(row8 < j, pltpu.roll(prev8, j, 0), rolled[:SUBLANES])
        xj = jnp.concatenate([head, rolled[SUBLANES:]], axis=0)
        xc = xc + xj * cw[CONV_W - 1 - j:CONV_W - j]

    a, b, gg = _lru_gates(xc, gate_ref[...], wa_ref, wx_ref, ba_ref, bx_ref, lam_ref)

    row = lax.broadcasted_iota(jnp.int32, a.shape, 0)
    s = 1
    while s < tt:
        keep = row >= s
        b = jnp.where(keep, a * pltpu.roll(b, s, 0) + b, b)
        a = jnp.where(keep, a * pltpu.roll(a, s, 0), a)
        s *= 2
    h = a * hc_ref[...] + b

    y_ref[...] = (h * gg).astype(y_ref.dtype)
    tail_ref[0] = xb[tt - SUBLANES:]
    h_ref[0] = h[tt - 1:]
    prev_ref[...] = xb[tt - SUBLANES:]
    hc_ref[...] = h[tt - 1:]


def lru_prompt(proj, p, j, bsz, t_len, tt=256):
    d = proj.shape[1] // 2
    nb, bw = p['lru_w_a'].shape[1], p['lru_w_a'].shape[2]
    tt = min(tt, t_len)
    assert t_len % tt == 0 and tt % SUBLANES == 0
    nc = t_len // tt
    vec = lambda a: a[j].reshape(1, d)
    vspec = pl.BlockSpec((1, bw), lambda b, n, c: (0, n))
    wspec = pl.BlockSpec((1, bw, bw), lambda b, n, c: (n, 0, 0))
    return pl.pallas_call(
        _lru_prompt_body,
        grid=(bsz, nb, nc),
        in_specs=[
            pl.BlockSpec((tt, bw), lambda b, n, c: (b * nc + c, n)),
            pl.BlockSpec((tt, bw), lambda b, n, c: (b * nc + c, nb + n)),
            pl.BlockSpec((CONV_W, bw), lambda b, n, c: (0, n)),
            vspec, wspec, wspec, vspec, vspec, vspec,
        ],
        out_specs=[
            pl.BlockSpec((tt, bw), lambda b, n, c: (b * nc + c, n)),
            pl.BlockSpec((1, SUBLANES, bw), lambda b, n, c: (b, 0, n)),
            pl.BlockSpec((1, 1, bw), lambda b, n, c: (b, 0, n)),
        ],
        out_shape=[
            jax.ShapeDtypeStruct((bsz * t_len, d), BF16),
            jax.ShapeDtypeStruct((bsz, SUBLANES, d), F32),
            jax.ShapeDtypeStruct((bsz, 1, d), F32),
        ],
        scratch_shapes=[pltpu.VMEM((SUBLANES, bw), F32), pltpu.VMEM((1, bw), F32)],
        compiler_params=_params("arbitrary", "arbitrary", "arbitrary"),
    )(proj, proj, p['lru_conv_w'][j], vec(p['lru_conv_b']), p['lru_w_a'][j], p['lru_w_x'][j],
      vec(p['lru_b_a']), vec(p['lru_b_x']), vec(p['lru_lam']))


def _lru_sample_body(xb_ref, gate_ref, cst_ref, h0_ref, cw_ref, cb_ref, wa_ref, wx_ref, ba_ref, bx_ref,
                     lam_ref, y_ref, tail_ref, h_ref, *, t_len, nseq):
    cw = cw_ref[...]
    xs = [cst_ref[q] for q in range(CONV_W - 1)]
    xs += [xb_ref[t * nseq:(t + 1) * nseq] for t in range(t_len)]
    xcs = []
    for t in range(t_len):
        xc = cb_ref[...] + xs[t] * cw[0:1]
        for q in range(1, CONV_W):
            xc = xc + xs[t + q] * cw[q:q + 1]
        xcs.append(xc)
    a, b, gg = _lru_gates(jnp.concatenate(xcs, axis=0), gate_ref[...], wa_ref, wx_ref, ba_ref, bx_ref, lam_ref)
    h = h0_ref[...]
    for t in range(t_len):
        sl = slice(t * nseq, (t + 1) * nseq)
        h = a[sl] * h + b[sl]
        y_ref[sl, :] = (h * gg[sl]).astype(y_ref.dtype)
    h_ref[...] = h
    for q in range(CONV_W - 1):
        tail_ref[q] = xs[t_len + q]


def lru_sample(proj, conv_st, h0, p, j, mp, nseq, t_len):
    d = proj.shape[1] // 2
    nb, bw = p['lru_w_a'].shape[1], p['lru_w_a'].shape[2]
    ms = nseq * t_len
    assert mp % ms == 0
    blk0 = mp // ms
    vec = lambda a: a[j].reshape(1, d)
    vspec = pl.BlockSpec((1, bw), lambda n: (0, n))
    wspec = pl.BlockSpec((1, bw, bw), lambda n: (n, 0, 0))
    return pl.pallas_call(
        functools.partial(_lru_sample_body, t_len=t_len, nseq=nseq),
        grid=(nb,),
        in_specs=[
            pl.BlockSpec((ms, bw), lambda n: (blk0, n)),
            pl.BlockSpec((ms, bw), lambda n: (blk0, nb + n)),
            pl.BlockSpec((CONV_W - 1, nseq, bw), lambda n: (0, 0, n)),
            pl.BlockSpec((nseq, bw), lambda n: (0, n)),
            pl.BlockSpec((CONV_W, bw), lambda n: (0, n)),
            vspec, wspec, wspec, vspec, vspec, vspec,
        ],
        out_specs=[
            pl.BlockSpec((ms, bw), lambda n: (0, n)),
            pl.BlockSpec((CONV_W - 1, nseq, bw), lambda n: (0, 0, n)),
            pl.BlockSpec((nseq, bw), lambda n: (0, n)),
        ],
        out_shape=[
            jax.ShapeDtypeStruct((ms, d), BF16),
            jax.ShapeDtypeStruct((CONV_W - 1, nseq, d), F32),
            jax.ShapeDtypeStruct((nseq, d), F32),
        ],
        compiler_params=_params("arbitrary"),
    )(proj, proj, conv_st, h0, p['lru_conv_w'][j], vec(p['lru_conv_b']), p['lru_w_a'][j], p['lru_w_x'][j],
      vec(p['lru_b_a']), vec(p['lru_b_x']), vec(p['lru_lam']))


def _mix_store(x, prev, mu_ref, out_refs):
    xx = prev - x
    for q, o_ref in enumerate(out_refs):
        o_ref[...] = (x + xx * mu_ref[q:q + 1]).astype(o_ref.dtype)


def _shift_prompt_body(x_ref, mu_ref, *rest, blocks_per_seq):
    out_refs, carry_ref = rest[:-1], rest[-1]
    i = pl.program_id(1)
    x = x_ref[...]
    first = jnp.where(i % blocks_per_seq == 0, jnp.zeros_like(carry_ref[...]), carry_ref[...])
    row = lax.broadcasted_iota(jnp.int32, x.shape, 0)
    prev = jnp.where(row == 0, first, pltpu.roll(x, 1, 0))
    _mix_store(x, prev, mu_ref, out_refs)
    carry_ref[...] = x[x.shape[0] - 1:]


def shift_prompt(x, mu, mp, t_len, tm=256, tc=1024):
    d = x.shape[1]
    tm, tc = min(tm, t_len), min(tc, d)
    assert t_len % tm == 0 and d % tc == 0
    nq = mu.shape[0]
    spec = pl.BlockSpec((tm, tc), lambda c, i: (i, c))
    return pl.pallas_call(
        functools.partial(_shift_prompt_body, blocks_per_seq=t_len // tm),
        grid=(d // tc, mp // tm),
        in_specs=[spec, pl.BlockSpec((nq, tc), lambda c, i: (0, c))],
        out_specs=[spec] * nq,
        out_shape=[jax.ShapeDtypeStruct((mp, d), BF16)] * nq,
        scratch_shapes=[pltpu.VMEM((1, tc), F32)],
        compiler_params=_params("arbitrary", "arbitrary"),
    )(x, mu)


def _shift_sample_body(x_ref, st_ref, mu_ref, *out_refs, nseq):
    x = x_ref[...]
    prev = jnp.concatenate([st_ref[...], x[:x.shape[0] - nseq]], axis=0)
    _mix_store(x, prev, mu_ref, out_refs)


def shift_sample(x, state, mu, mp, nseq, t_len, tc=1024):
    d = x.shape[1]
    tc = min(tc, d)
    ms = nseq * t_len
    blk0 = mp // ms
    nq = mu.shape[0]
    return pl.pallas_call(
        functools.partial(_shift_sample_body, nseq=nseq),
        grid=(d // tc,),
        in_specs=[pl.BlockSpec((ms, tc), lambda c: (blk0, c)),
                  pl.BlockSpec((nseq, tc), lambda c: (0, c)),
                  pl.BlockSpec((nq, tc), lambda c: (0, c))],
        out_specs=[pl.BlockSpec((ms, tc), lambda c: (0, c))] * nq,
        out_shape=[jax.ShapeDtypeStruct((ms, d), BF16)] * nq,
        compiler_params=_params("arbitrary"),
    )(x, state, mu)


def _seg_sum(x):
    lo = lax.broadcasted_iota(jnp.int32, x.shape, x.ndim - 1) < RWKV_HEAD
    s_lo = jnp.sum(jnp.where(lo, x, 0.0), axis=-1, keepdims=True)
    s_hi = jnp.sum(jnp.where(lo, 0.0, x), axis=-1, keepdims=True)
    return jnp.where(lo, s_lo, s_hi)


def _wkv_parse(refs, has_vres, has_state, n_scratch):
    it = iter(refs)
    acts = tuple(next(it) for _ in range(6))
    vres = tuple(next(it) for _ in range(3)) if has_vres else None
    vecs = tuple(next(it) for _ in range(7))
    s0_ref = next(it) if has_state else None
    y_ref, s_ref = next(it), next(it)
    scratch = tuple(next(it) for _ in range(n_scratch))
    return acts, vres, vecs, s0_ref, y_ref, s_ref, scratch


def _wkv_operands(acts, vres, vecs, n_pairs):
    r_ref, k_ref, v_ref, wr_ref, ar_ref, _ = acts
    w0_ref, a0_ref, kk_ref, ka_ref, rk_ref, _, _ = vecs
    r = r_ref[...]
    k = k_ref[...]
    v = v_ref[...]
    w = -_softplus(-(w0_ref[...] + wr_ref[...])) - 0.5
    a = _sigmoid(a0_ref[...] + ar_ref[...])
    if vres is not None:
        vr_ref, vf_ref, v0_ref = vres
        v = v + (vf_ref[...] - v) * _sigmoid(v0_ref[...] + vr_ref[...])
    k2 = k * (1.0 + (a - 1.0) * ka_ref[...])
    kks = []
    for pr in range(n_pairs):
        ls = slice(pr * PAIR, (pr + 1) * PAIR)
        kkr = k[:, ls] * kk_ref[:, ls]
        kks.append(kkr / jnp.maximum(jnp.sqrt(_seg_sum(kkr * kkr)), 1e-12))
    kk = kks[0] if n_pairs == 1 else jnp.concatenate(kks, axis=1)
    return r, v, k2, kk, kk * a, -jnp.exp(w), r * k2 * rk_ref[...]


def _wkv_finish(ys_s, bonus_terms, v, acts, vecs, y_ref, n_pairs):
    g_ref, lg_ref, lb_ref = acts[5], vecs[5], vecs[6]
    for pr in range(n_pairs):
        ls = slice(pr * PAIR, (pr + 1) * PAIR)
        y = ys_s[:, ls]
        mu = _seg_sum(y) * (1.0 / RWKV_HEAD)
        yc = y - mu
        var = _seg_sum(yc * yc) * (1.0 / RWKV_HEAD)
        yn = yc * lax.rsqrt(var + GN_EPS) * lg_ref[:, ls] + lb_ref[:, ls]
        bonus = _seg_sum(bonus_terms[:, ls]) * v[:, ls]
        y_ref[:, ls] = ((yn + bonus) * g_ref[:, ls]).astype(y_ref.dtype)


def _wkv_steps_body(*refs, nb, nt, n_pairs, has_vres):
    acts, vres, vecs, s0_ref, y_ref, s_ref, scratch = _wkv_parse(refs, has_vres, True, 6)
    dec_s, kkn_s, kab_s, k2_s, v_s, ys_s = scratch
    r_ref = acts[0]
    s_ref[...] = s0_ref[...]
    _, v, k2, kk, kka, logdec, bonus_terms = _wkv_operands(acts, vres, vecs, n_pairs)
    dec_s[...] = jnp.exp(logdec)
    kkn_s[...] = -kk
    kab_s[...] = kka
    k2_s[...] = k2
    v_s[...] = v

    eye = (lax.broadcasted_iota(jnp.int32, (RWKV_HEAD, PAIR), 1) % RWKV_HEAD
           == lax.broadcasted_iota(jnp.int32, (RWKV_HEAD, PAIR), 0))
    sub8 = lax.broadcasted_iota(jnp.int32, (SUBLANES, PAIR), 0)
    operand_refs = (kkn_s, v_s, dec_s, kab_s, k2_s, r_ref)

    half_sum = ((lax.broadcasted_iota(jnp.int32, (2 * PAIR, PAIR), 0) % PAIR) // RWKV_HEAD
                == lax.broadcasted_iota(jnp.int32, (2 * PAIR, PAIR), 1) // RWKV_HEAD).astype(BF16)

    def seg_sums(xs):
        cat = jnp.concatenate(xs, axis=0)
        hi = cat.astype(BF16)
        lo = (cat - hi.astype(F32)).astype(BF16)
        out = jnp.dot(jnp.concatenate([hi, lo], axis=1), half_sum, preferred_element_type=F32)
        return [out[q * RWKV_HEAD:(q + 1) * RWKV_HEAD] for q in range(len(xs))]

    chains = range(SUBLANES)
    for t in range(nt):
        def seq_group(gi, carry):
            b0 = gi * SUBLANES
            rows = pl.ds(pl.multiple_of(t * nb + b0, SUBLANES), SUBLANES)
            for pr in range(n_pairs):
                ls = slice(pr * PAIR, (pr + 1) * PAIR)
                kkn, vv, dec, kab, k2r, rr = (ref[rows, ls] for ref in operand_refs)
                s = [s_ref[b0 + u, pr] for u in chains]
                sums = seg_sums([s[u] * kkn[u:u + 1] for u in chains]
                                + [jnp.where(eye, vv[u:u + 1], 0.0) for u in chains])
                s = [s[u] * dec[u:u + 1] + sums[u] * kab[u:u + 1] + sums[SUBLANES + u] * k2r[u:u + 1]
                     for u in chains]
                ycol = seg_sums([s[u] * rr[u:u + 1] for u in chains])
                ytile = jnp.zeros((SUBLANES, PAIR), F32)
                for u in chains:
                    s_ref[b0 + u, pr] = s[u]
                    yrow = jnp.sum(jnp.where(eye, ycol[u], 0.0), axis=0, keepdims=True)
                    ytile = jnp.where(sub8 == u, yrow, ytile)
                ys_s[rows, ls] = ytile
            return carry

        lax.fori_loop(0, nb // SUBLANES, seq_group, 0)

    _wkv_finish(ys_s, bonus_terms, v, acts, vecs, y_ref, n_pairs)


def _bdot(a, b, dims=(((1,), (0,)), ((), ()))):
    return lax.dot_general(a.astype(BF16), b.astype(BF16), dims, preferred_element_type=F32)


_NT = (((1,), (1,)), ((), ()))
_TN = (((0,), (0,)), ((), ()))


def _wkv_chunk_body(*refs, n_pairs, has_vres, chunk):
    acts, vres, vecs, _, y_ref, z_ref, scratch = _wkv_parse(refs, has_vres, False, 6)
    ld_s, kk_s, ka_s, k2_s, v_s, ys_s = scratch
    r_ref = acts[0]

    @pl.when(pl.program_id(2) == 0)
    def _():
        z_ref[...] = jnp.zeros_like(z_ref)

    _, v, k2, kk, kka, logdec, bonus_terms = _wkv_operands(acts, vres, vecs, n_pairs)
    ld_s[...] = logdec
    kk_s[...] = kk
    ka_s[...] = kka
    k2_s[...] = k2
    v_s[...] = v

    row_c = lax.broadcasted_iota(jnp.int32, (chunk, PAIR), 0)
    lo_c = lax.broadcasted_iota(jnp.int32, (chunk, PAIR), 1) < RWKV_HEAD
    i2 = lax.broadcasted_iota(jnp.int32, (2 * chunk, 2 * chunk), 0)
    j2 = lax.broadcasted_iota(jnp.int32, (2 * chunk, 2 * chunk), 1)
    same = (i2 // chunk) == (j2 // chunk)
    strict = same & ((i2 % chunk) > (j2 % chunk))
    incl = same & ((i2 % chunk) >= (j2 % chunk))
    ik = lax.broadcasted_iota(jnp.int32, (PAIR, PAIR), 0)
    jk = lax.broadcasted_iota(jnp.int32, (PAIR, PAIR), 1)
    n2 = 2 * chunk

    def stack2(x):
        return jnp.concatenate([jnp.where(lo_c, x, 0.0), jnp.where(lo_c, 0.0, x)], axis=0)

    def chunk_step(ci, carry):
        rows = pl.ds(pl.multiple_of(ci * chunk, chunk), chunk)
        pairs = range(n_pairs)
        lss = [slice(pr * PAIR, (pr + 1) * PAIR) for pr in pairs]
        qk, qr, kcat, kag, kkg, vb, g_col = ([] for _ in range(7))
        for ls in lss:
            ld = ld_s[rows, ls]
            cum = ld
            s = 1
            while s < chunk:
                cum = cum + jnp.where(row_c >= s, pltpu.roll(cum, s, 0), 0.0)
                s *= 2
            lwc = cum[chunk - 1:chunk]
            kk_c, ka_c, k2_c = kk_s[rows, ls], ka_s[rows, ls], k2_s[rows, ls]
            e_inv = jnp.exp(-cum)
            e_end = jnp.exp(lwc - cum)
            qk.append(stack2(kk_c * jnp.exp(cum - ld)).astype(BF16))
            qr.append(stack2(r_ref[rows, ls] * jnp.exp(cum)))
            kcat.append(jnp.concatenate([stack2(ka_c * e_inv), stack2(k2_c * e_inv)], axis=0).astype(BF16))
            kag.append(stack2(ka_c * e_end).astype(BF16))
            kkg.append(stack2(k2_c * e_end).astype(BF16))
            vb.append(stack2(v_s[rows, ls]).astype(BF16))
            g_col.append(jnp.sum(jnp.where(ik == jk, jnp.exp(lwc), 0.0), axis=1, keepdims=True))
        sc = [_bdot(jnp.concatenate([qk[p], qr[p].astype(BF16)], axis=0), kcat[p], _NT) for p in pairs]
        a_ak = [jnp.where(strict, sc[p][:n2, :n2], 0.0) for p in pairs]
        a_kk = [jnp.where(strict, sc[p][:n2, n2:], 0.0).astype(BF16) for p in pairs]
        b_ak = [jnp.where(incl, sc[p][n2:, :n2], 0.0).astype(BF16) for p in pairs]
        b_kk = [jnp.where(incl, sc[p][n2:, n2:], 0.0).astype(BF16) for p in pairs]
        tinv = [jnp.where(i2 == j2, 1.0, 0.0) - a for a in a_ak]
        apow = [a.astype(BF16) for a in a_ak]
        s = 1
        while 2 * s < chunk:
            apow = [_bdot(a, a).astype(BF16) for a in apow]
            tinv = [t + _bdot(t, a) for t, a in zip(tinv, apow)]
            s *= 2
        w = [_bdot(a_kk[p], vb[p]).astype(BF16) for p in pairs]
        pu = [(-_bdot(tinv[p], jnp.concatenate([qk[p], w[p]], axis=1))).astype(BF16) for p in pairs]
        bp = [_bdot(b_ak[p], pu[p]) for p in pairs]
        kp = [_bdot(kag[p], pu[p], _TN) for p in pairs]
        y0 = [bp[p][:, PAIR:] + _bdot(b_kk[p], vb[p]) for p in pairs]
        z0 = [kp[p][:, PAIR:] + _bdot(kkg[p], vb[p], _TN) for p in pairs]
        for p in pairs:
            z = z_ref[0, p]
            zb = z.astype(BF16)
            yb = _bdot(qr[p] + bp[p][:, :PAIR], zb) + y0[p]
            ys_s[rows, lss[p]] = yb[:chunk] + yb[chunk:]
            z_ref[0, p] = g_col[p] * z + _bdot(kp[p][:, :PAIR], zb) + z0[p]
        return carry

    lax.fori_loop(0, r_ref.shape[0] // chunk, chunk_step, 0)
    _wkv_finish(ys_s, bonus_terms, v, acts, vecs, y_ref, n_pairs)


def _wkv_call(body, acts, vres, p, j, state, *, grid, rows, n_pairs, act_map, out_map, vec_map, st_map,
              out_rows, state_shape, state_block):
    d = acts[0].shape[1]
    lw = n_pairs * PAIR
    aspec = pl.BlockSpec((rows, lw), act_map)
    vspec = pl.BlockSpec((1, lw), vec_map)
    sspec = pl.BlockSpec(state_block, st_map)
    vec = lambda a: a.reshape(1, d)
    args, in_specs = list(acts), [aspec] * 6
    if vres is not None:
        v_raw, v_first, v0 = vres
        args += [v_raw, v_first, vec(v0)]
        in_specs += [aspec, aspec, vspec]
    args += [vec(p['rw_w0'][j]), vec(p['rw_a0'][j]), vec(p['rw_k_k'][j]), vec(p['rw_k_a'][j]),
             vec(p['rw_r_k'][j]), vec(p['rw_lnx_g'][j]), vec(p['rw_lnx_b'][j])]
    in_specs += [vspec] * 7
    if state is not None:
        args.append(state)
        in_specs.append(sspec)
    return pl.pallas_call(
        body,
        grid=grid,
        in_specs=in_specs,
        out_specs=[pl.BlockSpec((rows, lw), out_map), sspec],
        out_shape=[jax.ShapeDtypeStruct((out_rows, d), BF16), jax.ShapeDtypeStruct(state_shape, F32)],
        scratch_shapes=[pltpu.VMEM((rows, lw), F32)] * 6,
        compiler_params=_params(*(["arbitrary"] * len(grid))),
    )(*args)


def wkv_prompt(acts, vres, p, j, bsz, t_len, n_pairs=8, rows=256, chunk=64):
    d = acts[0].shape[1]
    n_pairs = min(n_pairs, d // PAIR)
    rows = min(rows, t_len)
    assert t_len % rows == 0 and rows % chunk == 0
    nc = t_len // rows
    return _wkv_call(
        functools.partial(_wkv_chunk_body, n_pairs=n_pairs, has_vres=vres is not None, chunk=chunk),
        acts, vres, p, j, None, grid=(bsz, d // (n_pairs * PAIR), nc), rows=rows, n_pairs=n_pairs,
        act_map=lambda b, l, c: (b * nc + c, l), out_map=lambda b, l, c: (b * nc + c, l),
        vec_map=lambda b, l, c: (0, l), st_map=lambda b, l, c: (b, l, 0, 0),
        out_rows=bsz * t_len, state_shape=(bsz, d // PAIR, PAIR, PAIR), state_block=(1, n_pairs, PAIR, PAIR))


def wkv_sample(acts, vres, p, j, state, row_blk0, nseq, t_len):
    d = acts[0].shape[1]
    return _wkv_call(
        functools.partial(_wkv_steps_body, nb=nseq, nt=t_len, n_pairs=1, has_vres=vres is not None),
        acts, vres, p, j, state, grid=(d // PAIR,), rows=nseq * t_len, n_pairs=1,
        act_map=lambda l: (row_blk0, l), out_map=lambda l: (0, l), vec_map=lambda l: (0, l),
        st_map=lambda l: (0, l, 0, 0), out_rows=nseq * t_len, state_shape=(nseq, d // PAIR, RWKV_HEAD, PAIR),
        state_block=(nseq, 1, RWKV_HEAD, PAIR))


def _pair_state(s):
    b, h, nv, nk = s.shape
    return s.reshape(b, h // 2, 2, nv, nk).transpose(0, 1, 3, 2, 4).reshape(b, h // 2, nv, 2 * nk)


def _unpair_state(s):
    b, hp, nv, nk2 = s.shape
    return s.reshape(b, hp, nv, 2, nk2 // 2).transpose(0, 1, 3, 2, 4).reshape(b, hp * 2, nv, nk2 // 2)


def _state_from_blockdiag(z):
    b, hp = z.shape[:2]
    z = z.reshape(b, hp, 2, RWKV_HEAD, 2, RWKV_HEAD)
    diag = jnp.stack([z[:, :, 0, :, 0, :], z[:, :, 1, :, 1, :]], axis=2)
    return diag.transpose(0, 1, 2, 4, 3).reshape(b, hp * 2, RWKV_HEAD, RWKV_HEAD)


def _route(logits, n_groups, n_experts, tm):
    m = logits.shape[0]
    epg = n_experts // n_groups
    g_logits = logits[:, :n_groups]
    g_idx = jnp.argmax(g_logits, axis=-1)
    p_group = jnp.max(jax.nn.softmax(g_logits, axis=-1), axis=-1, keepdims=True)
    e_logits = logits[:, n_groups:n_groups + n_experts].reshape(m, n_groups, epg)
    e_in = jnp.take_along_axis(e_logits, g_idx[:, None, None], axis=1)[:, 0]
    top_val, top_idx = lax.top_k(e_in, TOP_K)
    gate = p_group * jax.nn.softmax(top_val, axis=-1)
    expert_flat = (g_idx[:, None] * epg + top_idx).astype(jnp.int32).reshape(-1)

    n_assign = m * TOP_K
    order = jnp.argsort(expert_flat).astype(jnp.int32)
    e_sorted = expert_flat[order]
    tok_sorted = order // TOP_K
    counts = jnp.sum(expert_flat[:, None] == jnp.arange(n_experts, dtype=jnp.int32)[None, :], axis=0, dtype=jnp.int32)
    start = jnp.cumsum(counts) - counts
    nblk_e = (counts + tm - 1) // tm
    blk_end = jnp.cumsum(nblk_e)
    blk_start = blk_end - nblk_e
    slot_sorted = blk_start[e_sorted] * tm + (jnp.arange(n_assign, dtype=jnp.int32) - start[e_sorted])
    slot_of_assign = slot_sorted[jnp.argsort(order)]
    n_blocks = (n_assign + n_experts * (tm - 1) + tm - 1) // tm
    blk = jnp.arange(n_blocks, dtype=jnp.int32)
    blk_expert = jnp.minimum(jnp.sum(blk[:, None] >= blk_end[None, :], axis=1), n_experts - 1)
    e_slot = jnp.repeat(blk_expert, tm)
    off = jnp.arange(n_blocks * tm, dtype=jnp.int32) - blk_start[e_slot] * tm
    valid = (off >= 0) & (off < counts[e_slot])
    slot_tok = jnp.where(valid, tok_sorted[jnp.clip(start[e_slot] + off, 0, n_assign - 1)], 0)
    return slot_tok, slot_of_assign.reshape(m, TOP_K), gate.astype(F32), blk_start, nblk_e


def _gather_body(tok_ref, x_hbm, o_ref, buf_ref, sem, *, tm):
    i = pl.program_id(0)

    def copy(step, r):
        slot = step % 2
        return pltpu.make_async_copy(x_hbm.at[pl.ds(tok_ref[step * tm + r], 1)],
                                     buf_ref.at[slot, pl.ds(r, 1)], sem.at[slot])

    def issue(step):
        def pair(h, c):
            copy(step, 2 * h).start(priority=0)
            copy(step, 2 * h + 1).start(priority=1)
            return c

        lax.fori_loop(0, tm // 2, pair, 0)

    @pl.when(i == 0)
    def _():
        issue(0)

    @pl.when(i + 1 < pl.num_programs(0))
    def _():
        issue(i + 1)

    lax.fori_loop(0, tm, lambda r, c: (copy(i, r).wait(), c)[1], 0)
    o_ref[...] = buf_ref[i % 2].astype(o_ref.dtype)


def gather_rows(x, slot_tok, tm=256):
    d = x.shape[1]
    n_slots = slot_tok.shape[0]
    tm = _div_tile(n_slots, tm)
    return pl.pallas_call(
        functools.partial(_gather_body, tm=tm),
        grid_spec=pltpu.PrefetchScalarGridSpec(
            num_scalar_prefetch=1,
            grid=(n_slots // tm,),
            in_specs=[pl.BlockSpec(memory_space=pl.ANY)],
            out_specs=pl.BlockSpec((tm, d), lambda i, tok: (i, 0)),
            scratch_shapes=[pltpu.VMEM((2, tm, d), F32), pltpu.SemaphoreType.DMA((2,))],
        ),
        out_shape=jax.ShapeDtypeStruct((n_slots, d), BF16),
        compiler_params=_params("arbitrary"),
    )(slot_tok, x)


def _expert_blocks(bs_ref, nb_ref, src_hbm, dst_hbm, xbuf, obuf, xsem, osem, compute, *, tm, dst_cols):
    e = pl.program_id(0)
    row0 = bs_ref[e] * tm
    n = nb_ref[e]

    def rows(b):
        return pl.ds(pl.multiple_of(row0 + b * tm, tm), tm)

    def xcopy(b):
        return pltpu.make_async_copy(src_hbm.at[rows(b)], xbuf.at[b % 2], xsem.at[b % 2])

    def ocopy(b):
        dst = dst_hbm.at[rows(b)] if dst_cols is None else dst_hbm.at[rows(b), dst_cols]
        return pltpu.make_async_copy(obuf.at[b % 2], dst, osem.at[b % 2])

    @pl.when(n > 0)
    def _():
        xcopy(0).start()

    def block(b, c):
        xcopy(b).wait()

        @pl.when(b + 1 < n)
        def _():
            xcopy(b + 1).start()

        @pl.when(b >= 2)
        def _():
            ocopy(b - 2).wait()

        obuf[b % 2] = compute(xbuf[b % 2]).astype(obuf.dtype)
        ocopy(b).start()
        return c

    lax.fori_loop(0, n, block, 0)

    @pl.when(n >= 2)
    def _():
        ocopy(n - 2).wait()

    @pl.when(n >= 1)
    def _():
        ocopy(n - 1).wait()

    @pl.when(e == pl.num_programs(0) - 1)
    def _():
        obuf[0] = jnp.zeros(obuf.shape[1:], obuf.dtype)

        def fill(blk, c):
            fill_rows = pl.ds(pl.multiple_of(blk * tm, tm), tm)
            dst = dst_hbm.at[fill_rows] if dst_cols is None else dst_hbm.at[fill_rows, dst_cols]
            cp = pltpu.make_async_copy(obuf.at[0], dst, osem.at[0])
            cp.start()
            cp.wait()
            return c

        lax.fori_loop(bs_ref[e] + n, dst_hbm.shape[0] // tm, fill, 0)


def _ffn_up_body(bs_ref, nb_ref, x_hbm, wg_ref, wu_ref, o_hbm, wgbf_ref, wubf_ref, xbuf, obuf, xsem, osem, *, tm, tn):
    wgbf_ref[...] = wg_ref[...].astype(BF16)
    wubf_ref[...] = wu_ref[...].astype(BF16)

    def compute(x):
        hg = jnp.dot(x, wgbf_ref[...], preferred_element_type=F32)
        hu = jnp.dot(x, wubf_ref[...], preferred_element_type=F32)
        return hg * _sigmoid(hg) * hu

    cols = pl.ds(pl.multiple_of(pl.program_id(1) * tn, tn), tn)
    _expert_blocks(bs_ref, nb_ref, x_hbm, o_hbm, xbuf, obuf, xsem, osem, compute, tm=tm, dst_cols=cols)


def _ffn_down_body(bs_ref, nb_ref, h_hbm, wd_ref, o_hbm, wdbf_ref, xbuf, obuf, xsem, osem, *, tm):
    wdbf_ref[...] = wd_ref[...].astype(BF16)

    def compute(h):
        return jnp.dot(h, wdbf_ref[...], preferred_element_type=F32)

    _expert_blocks(bs_ref, nb_ref, h_hbm, o_hbm, xbuf, obuf, xsem, osem, compute, tm=tm, dst_cols=None)


def expert_ffn(x_sorted, blk_start, nblk_e, w_gate, w_up, w_down, layer, tm, tn_up=512):
    n_slots, d = x_sorted.shape
    n_experts, de = w_gate.shape[1], w_gate.shape[3]
    tn_up = min(tn_up, de)
    any_spec = pl.BlockSpec(memory_space=pl.ANY)
    dma2 = pltpu.SemaphoreType.DMA((2,))
    hid = pl.pallas_call(
        functools.partial(_ffn_up_body, tm=tm, tn=tn_up),
        grid_spec=pltpu.PrefetchScalarGridSpec(
            num_scalar_prefetch=2,
            grid=(n_experts, de // tn_up),
            in_specs=[any_spec,
                      pl.BlockSpec((None, None, d, tn_up), lambda e, j, bs, nb: (layer, e, 0, j)),
                      pl.BlockSpec((None, None, d, tn_up), lambda e, j, bs, nb: (layer, e, 0, j))],
            out_specs=any_spec,
            scratch_shapes=[pltpu.VMEM((d, tn_up), BF16), pltpu.VMEM((d, tn_up), BF16),
                            pltpu.VMEM((2, tm, d), BF16), pltpu.VMEM((2, tm, tn_up), BF16), dma2, dma2],
        ),
        out_shape=jax.ShapeDtypeStruct((n_slots, de), BF16),
        compiler_params=_params("arbitrary", "arbitrary"),
    )(blk_start, nblk_e, x_sorted, w_gate, w_up)
    return pl.pallas_call(
        functools.partial(_ffn_down_body, tm=tm),
        grid_spec=pltpu.PrefetchScalarGridSpec(
            num_scalar_prefetch=2,
            grid=(n_experts,),
            in_specs=[any_spec, pl.BlockSpec((None, None, de, d), lambda e, bs, nb: (layer, e, 0, 0))],
            out_specs=any_spec,
            scratch_shapes=[pltpu.VMEM((de, d), BF16), pltpu.VMEM((2, tm, de), BF16),
                            pltpu.VMEM((2, tm, d), F32), dma2, dma2],
        ),
        out_shape=jax.ShapeDtypeStruct((n_slots, d), F32),
        compiler_params=_params("arbitrary"),
    )(blk_start, nblk_e, hid, w_down)


def _combine_ln_body(slot_ref, x_ref, gate_ref, g_ref, b_ref, y_hbm, o_ref, obf_ref, buf_ref, sem, *, tm, alpha):
    i = pl.program_id(0)

    def copy(step, q):
        slot = step % 2
        return pltpu.make_async_copy(y_hbm.at[pl.ds(slot_ref[step * tm * TOP_K + q], 1)],
                                     buf_ref.at[slot, q % TOP_K, pl.ds(q // TOP_K, 1)], sem.at[slot])

    def issue(step):
        def pair(h, c):
            copy(step, 2 * h).start(priority=0)
            copy(step, 2 * h + 1).start(priority=1)
            return c

        lax.fori_loop(0, tm * TOP_K // 2, pair, 0)

    @pl.when(i == 0)
    def _():
        issue(0)

    @pl.when(i + 1 < pl.num_programs(0))
    def _():
        issue(i + 1)

    lax.fori_loop(0, tm * TOP_K, lambda q, c: (copy(i, q).wait(), c)[1], 0)
    gate = gate_ref[...]
    buf = buf_ref.at[i % 2]
    ffn = buf[0] * gate[:, 0:1]
    for q in range(1, TOP_K):
        ffn = ffn + buf[q] * gate[:, q:q + 1]
    y = _layer_norm_rows(alpha * x_ref[...] + ffn, g_ref[...], b_ref[...])
    o_ref[...] = y
    obf_ref[...] = y.astype(BF16)


def combine_layer_norm(x, out_slots, slot_of_assign, gate, g, b, alpha, tm=256):
    m, d = x.shape
    tm = _div_tile(m, tm)
    row = pl.BlockSpec((tm, d), lambda i, s: (i, 0))
    vec = pl.BlockSpec((1, d), lambda i, s: (0, 0))
    return pl.pallas_call(
        functools.partial(_combine_ln_body, tm=tm, alpha=alpha),
        grid_spec=pltpu.PrefetchScalarGridSpec(
            num_scalar_prefetch=1,
            grid=(m // tm,),
            in_specs=[row, pl.BlockSpec((tm, TOP_K), lambda i, s: (i, 0)), vec, vec,
                      pl.BlockSpec(memory_space=pl.ANY)],
            out_specs=[row, row],
            scratch_shapes=[pltpu.VMEM((2, TOP_K, tm, d), F32), pltpu.SemaphoreType.DMA((2,))],
        ),
        out_shape=[jax.ShapeDtypeStruct((m, d), F32), jax.ShapeDtypeStruct((m, d), BF16)],
        compiler_params=_params("arbitrary"),
    )(slot_of_assign.reshape(-1), x, gate, g.reshape(1, d), b.reshape(1, d), out_slots)


def _trunk(x_prompt, x_sample, state_conv, state_h, state_shift, state_wkv, p, moe_tm=128):
    bsz, t_len, d = x_prompt.shape
    nseq, dec_t, _ = x_sample.shape
    mp, ms = bsz * t_len, nseq * dec_t
    depth = p['ln_g'].shape[0]
    alpha = (2 * depth) ** 0.25
    n_groups = p['moe_w_group'].shape[2]
    n_experts = p['moe_w_expert'].shape[2]

    x = jnp.concatenate([x_prompt.reshape(mp, d), x_sample.transpose(1, 0, 2).reshape(ms, d)], axis=0)
    xbf = x.astype(BF16)
    conv_s_tm = state_conv.transpose(0, 2, 1, 3)

    conv_p, h_p, shift_p, wkv_p, conv_s, h_s, shift_s, wkv_s = ([] for _ in range(8))
    v_first = None
    for i in range(depth):
        j = i // 2
        if i % 2 == 0:
            proj = matmul(xbf, p['lru_w_in'], j, mp, ms)
            y_p, tail_p, hl_p = lru_prompt(proj, p, j, bsz, t_len)
            y_s, tail_s, hl_s = lru_sample(proj, conv_s_tm[j], state_h[j], p, j, mp, nseq, dec_t)
            mix = matmul((y_p, y_s), p['lru_w_out'], j, mp, ms)
            conv_p.append(tail_p[:, SUBLANES - (CONV_W - 1):])
            h_p.append(hl_p[:, 0])
            conv_s.append(tail_s.transpose(1, 0, 2))
            h_s.append(hl_s)
        else:
            mu = p['rw_mu'][j]
            mixed_p = shift_prompt(x, mu, mp, t_len)
            mixed_s = shift_sample(x, state_shift[j], mu, mp, nseq, dec_t)
            xr, xw, xk, xv, xa, xg = zip(mixed_p, mixed_s)
            r = matmul(xr, p['rw_w_r'], j, mp, ms)
            k = matmul(xk, p['rw_w_k'], j, mp, ms)
            v = matmul(xv, p['rw_w_v'], j, mp, ms)
            w_raw = lora(xw, p['rw_w1'][j], p['rw_w2'][j], mp, ms, "tanh")
            a_raw = lora(xa, p['rw_a1'][j], p['rw_a2'][j], mp, ms, "none")
            g = lora(xg, p['rw_g1'][j], p['rw_g2'][j], mp, ms, "sigmoid")
            if j == 0:
                vres, v_first = None, v
            else:
                v_raw = lora(xv, p['rw_v1'][j - 1], p['rw_v2'][j - 1], mp, ms, "none")
                vres = (v_raw, v_first, p['rw_v0'][j - 1])
            acts = (r, k, v, w_raw, a_raw, g)
            yg_p, z_p = wkv_prompt(acts, vres, p, j, bsz, t_len)
            yg_s, st_s = wkv_sample(acts, vres, p, j, _pair_state(state_wkv[j]), mp // ms, nseq, dec_t)
            mix = matmul((yg_p, yg_s), p['rw_w_o'], j, mp, ms)
            shift_p.append(x[t_len - 1:mp:t_len])
            shift_s.append(x[mp + (dec_t - 1) * nseq:])
            wkv_p.append(_state_from_blockdiag(z_p))
            wkv_s.append(_unpair_state(st_s))

        w_route = jnp.concatenate([p['moe_w_group'][i], p['moe_w_expert'][i]], axis=1)
        b_route = jnp.concatenate([p['moe_b_group'][i], p['moe_b_expert'][i]], axis=0)
        pad = ROUTE_PAD - w_route.shape[1]
        w_route = jnp.pad(w_route, ((0, 0), (0, pad)))
        b_route = jnp.pad(b_route, (0, pad)).reshape(1, ROUTE_PAD)
        x, xbf, logits = residual_layer_norm(x, mix, p['ln_g'][i, 0], p['ln_b'][i, 0], alpha,
                                             router=(w_route, b_route))
        slot_tok, slot_of_assign, gate, blk_start, nblk_e = _route(logits, n_groups, n_experts, moe_tm)
        x_sorted = gather_rows(x, slot_tok)
        out_slots = expert_ffn(x_sorted, blk_start, nblk_e, p['moe_w_gate'], p['moe_w_up'], p['moe_w_down'],
                               i, moe_tm)
        x, xbf = combine_layer_norm(x, out_slots, slot_of_assign, gate, p['ln_g'][i, 1], p['ln_b'][i, 1], alpha)

    y_prompt = x[:mp].reshape(bsz, t_len, d)
    y_sample = x[mp:].reshape(dec_t, nseq, d).transpose(1, 0, 2)
    return (y_prompt, y_sample, jnp.stack(conv_p), jnp.stack(h_p), jnp.stack(shift_p), jnp.stack(wkv_p),
            jnp.stack(conv_s), jnp.stack(h_s), jnp.stack(shift_s), jnp.stack(wkv_s))


def kernel(x_prompt, x_sample, state_conv, state_h, state_shift, state_wkv, ln_g, ln_b, lru_w_in, lru_conv_w, lru_conv_b, lru_w_a, lru_b_a, lru_w_x, lru_b_x, lru_lam, lru_w_out, rw_mu, rw_w_r, rw_w_k, rw_w_v, rw_w_o, rw_w0, rw_w1, rw_w2, rw_a0, rw_a1, rw_a2, rw_v0, rw_v1, rw_v2, rw_g1, rw_g2, rw_k_k, rw_k_a, rw_r_k, rw_lnx_g, rw_lnx_b, moe_w_group, moe_b_group, moe_w_expert, moe_b_expert, moe_w_gate, moe_w_up, moe_w_down):
    p = dict(ln_g=ln_g, ln_b=ln_b, lru_w_in=lru_w_in, lru_conv_w=lru_conv_w, lru_conv_b=lru_conv_b,
             lru_w_a=lru_w_a, lru_b_a=lru_b_a, lru_w_x=lru_w_x, lru_b_x=lru_b_x, lru_lam=lru_lam,
             lru_w_out=lru_w_out, rw_mu=rw_mu, rw_w_r=rw_w_r, rw_w_k=rw_w_k, rw_w_v=rw_w_v, rw_w_o=rw_w_o,
             rw_w0=rw_w0, rw_w1=rw_w1, rw_w2=rw_w2, rw_a0=rw_a0, rw_a1=rw_a1, rw_a2=rw_a2,
             rw_v0=rw_v0, rw_v1=rw_v1, rw_v2=rw_v2, rw_g1=rw_g1, rw_g2=rw_g2, rw_k_k=rw_k_k, rw_k_a=rw_k_a,
             rw_r_k=rw_r_k, rw_lnx_g=rw_lnx_g, rw_lnx_b=rw_lnx_b, moe_w_group=moe_w_group,
             moe_b_group=moe_b_group, moe_w_expert=moe_w_expert, moe_b_expert=moe_b_expert,
             moe_w_gate=moe_w_gate, moe_w_up=moe_w_up, moe_w_down=moe_w_down)
    return _trunk(x_prompt, x_sample, state_conv, state_h, state_shift, state_wkv, p)
```
